```python
import jax, jax.numpy as jnp
from jax import lax
import numpy as np

D_MODEL = 1024
BATCH = 1
SEQ = 16384
DEPTH = 4

HEAD_DIM = 64
RW_WIDTH = 3 * D_MODEL // 8
RW_HEADS = RW_WIDTH // HEAD_DIM
RW_DECAY_LORA = 64
RW_AAA_LORA = 64
RW_MV_LORA = 32
RW_GATE_LORA = 128
RW_GN_EPS = 64e-5
HG_WIDTH = D_MODEL // 4
HG_HEADS = HG_WIDTH // HEAD_DIM
HG_EXPAND = 128
HG_FDIM = HG_HEADS * HG_EXPAND
HG_GATE_FLOOR = 1e-30
GL_WIDTH = D_MODEL - RW_WIDTH - HG_WIDTH
GL_HEADS = 4
GL_VAL_DIM = GL_WIDTH // GL_HEADS
GL_KEY_DIM = GL_VAL_DIM // 2
GL_KWIDTH = GL_HEADS * GL_KEY_DIM
GL_GATE_LORA = 16
GL_GATE_NORMALIZER = 16.0
GL_CONV_WIDTH = 4
GL_CONV_CH = 2 * GL_KWIDTH + GL_WIDTH
D_MIX = RW_WIDTH + HG_WIDTH + GL_WIDTH
D_FF = 2816
FFN_RESIDUAL_WEIGHT = 0.5
CHUNK = 16
NORM_EPS = 1e-5
L2_EPS = 1e-12

RW_SPLITS = (RW_WIDTH, RW_WIDTH, RW_WIDTH, RW_DECAY_LORA, RW_AAA_LORA, RW_GATE_LORA)
HG_SPLITS = (HG_FDIM, HG_FDIM, HG_WIDTH, HG_WIDTH)
GL_SPLITS = (GL_KWIDTH, GL_KWIDTH, GL_WIDTH, GL_GATE_LORA, GL_WIDTH)
RW_COLS = sum(RW_SPLITS)
HG_COLS = sum(HG_SPLITS)
GL_COLS = sum(GL_SPLITS)
N_IN = RW_COLS + HG_COLS + GL_COLS

kernel_name = "hybrid_rwkv7_hgrn2_gla_macaron"

F32 = jnp.float32


def split_cols(z, sizes):
    return jnp.split(z, np.cumsum(sizes)[:-1].tolist(), axis=-1)


def to_heads(z, heads):
    return z.reshape(z.shape[:-1] + (heads, z.shape[-1] // heads))


def rms_norm(x, gain):
    x32 = x.astype(F32)
    y = x32 * lax.rsqrt(jnp.mean(x32 * x32, axis=-1, keepdims=True) + NORM_EPS)
    return (y * gain.astype(F32)).astype(x.dtype)


def head_rms_norm(o, gain):
    o32 = o.astype(F32)
    o32 = o32 * lax.rsqrt(jnp.mean(o32 * o32, axis=-1, keepdims=True) + NORM_EPS)
    return o32.reshape(o.shape[:-2] + (-1,)) * gain


def swiglu(h, w_gate, w_up, w_down):
    return (jax.nn.silu(h @ w_gate) * (h @ w_up)) @ w_down


def token_shift(z, mu):
    prev = jnp.pad(z[:, :-1], ((0, 0), (1, 0), (0, 0)))
    return z + (prev - z) * mu


def causal_depthwise_conv(z, w):
    return lax.conv_general_dilated(
        z, w.astype(z.dtype)[:, None, :], window_strides=(1,), padding=[(GL_CONV_WIDTH - 1, 0)],
        dimension_numbers=("NWC", "WIO", "NWC"), feature_group_count=z.shape[-1])


def chunk_gla(q, k, v, log_a):
    dtype = v.dtype
    bsz, seq, heads, dk = q.shape
    dv = v.shape[-1]
    n = seq // CHUNK

    def blocks(z):
        return jnp.moveaxis(z.astype(F32).reshape(bsz, n, CHUNK, heads, z.shape[-1]), 1, 0)

    causal = jnp.tril(jnp.ones((CHUNK, CHUNK), bool))[None, :, :, None, None]

    def step(s, inp):
        qc, kc, vc, lc = inp
        g = jnp.cumsum(lc, axis=1)
        diff = jnp.where(causal, g[:, :, None] - g[:, None], 0.0)
        decay = jnp.where(causal, jnp.exp(diff), 0.0)
        attn = jnp.einsum('bshc,bjhc,bsjhc->bhsj', qc, kc, decay)
        o = (jnp.einsum('bhsj,bjhv->bshv', attn, vc)
             + jnp.einsum('bshc,bhcv->bshv', qc * jnp.exp(g), s))
        g_last = g[:, -1]
        s = (jnp.exp(g_last)[..., None] * s
             + jnp.einsum('bjhc,bjhv->bhcv', kc * jnp.exp(g_last[:, None] - g), vc))
        return s, o

    s0 = jnp.zeros((bsz, heads, dk, dv), F32)
    _, o = lax.scan(step, s0, (blocks(q), blocks(k), blocks(v), blocks(log_a)))
    return jnp.moveaxis(o, 0, 1).reshape(bsz, seq, heads, dv).astype(dtype)


def rwkv7_chunk(r, log_w, k, v, alpha, beta):
    dtype = v.dtype
    bsz, seq, heads, dk = r.shape
    dv = v.shape[-1]
    n = seq // CHUNK

    def blocks(z):
        return jnp.swapaxes(z.astype(F32).reshape(bsz, n, CHUNK, heads, z.shape[-1]), 2, 3)

    r, log_w, k, v, alpha, beta = (blocks(z) for z in (r, log_w, k, v, alpha, beta))
    g = jnp.cumsum(log_w, axis=3)
    g_last = g[:, :, :, -1:]
    inv = jnp.exp(-g)
    k_in, a_in = k * inv, alpha * inv
    beta_d = beta * jnp.exp(g - log_w)
    r_d = r * jnp.exp(g)
    strict = jnp.tril(jnp.ones((CHUNK, CHUNK), bool), -1)
    incl = jnp.tril(jnp.ones((CHUNK, CHUNK), bool))

    def pair(lhs, rhs, mask):
        return jnp.where(mask, jnp.einsum('bnhtc,bnhjc->bnhtj', lhs, rhs), 0.0)

    a_bk, a_ba = pair(beta_d, k_in, strict), pair(beta_d, a_in, strict)
    a_rk, a_ra = pair(r_d, k_in, incl), pair(r_d, a_in, incl)
    rhs = jnp.concatenate([beta_d, a_bk @ v], axis=-1)
    sol = lax.linalg.triangular_solve(a_ba, rhs, left_side=True, lower=True, unit_diagonal=True)
    w_s, u_s = sol[..., :dk], sol[..., dk:]
    q_eff = r_d - a_ra @ w_s
    o_intra = a_rk @ v - a_ra @ u_s
    dec = jnp.exp(g_last - g)
    k_end, a_end = k * dec, alpha * dec
    lowrank = jnp.einsum('bnhjc,bnhjd->bnhcd', a_end, w_s)
    upd = (jnp.einsum('bnhjc,bnhjv->bnhcv', k_end, v)
           - jnp.einsum('bnhjc,bnhjv->bnhcv', a_end, u_s))
    chunk_decay = jnp.exp(g_last[:, :, :, 0])

    def step(s, inp):
        d, lr, up, qe = inp
        o = qe @ s
        s = d[..., None] * s - lr @ s + up
        return s, o

    s0 = jnp.zeros((bsz, heads, dk, dv), F32)
    xs = tuple(jnp.moveaxis(z, 1, 0) for z in (chunk_decay, lowrank, upd, q_eff))
    _, o_inter = lax.scan(step, s0, xs)
    o = o_intra + jnp.moveaxis(o_inter, 0, 1)
    return jnp.swapaxes(o, 2, 3).reshape(bsz, seq, heads, dv).astype(dtype)


def rwkv7_mixer(z, vres_z, v_first, mu, w0, w_up, a0, a_up, g_up, k_k, k_a, r_k, ln_w, ln_b,
                vres_mu, v0, vres_up):
    z = token_shift(z, mu)
    r, k, v, w_dn, a_dn, g_dn = split_cols(z, RW_SPLITS)
    log_w = -jnp.exp(-jax.nn.softplus(-(w0 + jnp.tanh(w_dn) @ w_up).astype(F32)) - 0.5)
    a = jax.nn.sigmoid(a0 + a_dn @ a_up)
    g = jax.nn.sigmoid(g_dn) @ g_up
    if vres_z is None:
        v_first = v
    else:
        vres_z = token_shift(vres_z, vres_mu)
        v = v + (v_first - v) * jax.nn.sigmoid(v0 + vres_z @ vres_up)
    kk = to_heads(k * k_k, RW_HEADS).astype(F32)
    kk = kk / jnp.maximum(jnp.sqrt(jnp.sum(kk * kk, axis=-1, keepdims=True)), L2_EPS)
    k = k * (1 + (a - 1) * k_a)
    rh, kh, vh, ah = (to_heads(t, RW_HEADS) for t in (r, k, v, a))
    y = rwkv7_chunk(rh, to_heads(log_w, RW_HEADS), kh, vh, kk * ah, kk).astype(F32)
    mean = jnp.mean(y, axis=-1, keepdims=True)
    var = jnp.mean(jnp.square(y - mean), axis=-1, keepdims=True)
    y = ((y - mean) * lax.rsqrt(var + RW_GN_EPS)).reshape(z.shape[:-1] + (RW_WIDTH,))
    y = y * ln_w + ln_b
    bonus = jnp.sum(rh * kh * r_k, axis=-1, keepdims=True) * vh
    out = (y + bonus.reshape(y.shape)) * g
    return out, v_first


def hgrn2_mixer(z, lb, norm_gain):
    q, f, i, g = split_cols(z, HG_SPLITS)
    f = f.astype(F32)
    lb = lb.astype(F32)
    forget = lb + (1 - lb) * jax.nn.sigmoid(f)
    log_f = jnp.log(jnp.maximum(forget, HG_GATE_FLOOR))
    k = (1 - lb) * jax.nn.sigmoid(-f)
    o = chunk_gla(to_heads(jax.nn.silu(q), HG_HEADS), to_heads(k, HG_HEADS),
                  to_heads(i, HG_HEADS), to_heads(log_f, HG_HEADS))
    return head_rms_norm(o, norm_gain) * jax.nn.silu(g)


def gla_mixer(z, conv_w, gate_up, gate_b, norm_gain):
    q, k, v, gate_dn, og = split_cols(z, GL_SPLITS)
    qkv = jax.nn.silu(causal_depthwise_conv(jnp.concatenate([q, k, v], axis=-1), conv_w))
    q, k, v = split_cols(qkv, (GL_KWIDTH, GL_KWIDTH, GL_WIDTH))
    log_a = jax.nn.log_sigmoid((gate_dn @ gate_up + gate_b).astype(F32)) / GL_GATE_NORMALIZER
    o = chunk_gla(to_heads(q * GL_KEY_DIM ** -0.5, GL_HEADS), to_heads(k, GL_HEADS),
                  to_heads(v, GL_HEADS), to_heads(log_a, GL_HEADS))
    return head_rms_norm(o, norm_gain) * jax.nn.silu(og)


def setup_inputs(seed: int = 0) -> dict:
    key = jax.random.key(seed)
    ks = iter(jax.random.split(key, 64))
    L, LV = DEPTH, DEPTH - 1

    def nrm(shape, scale):
        return scale * jax.random.normal(next(ks), shape, F32)

    def gain(shape):
        return 1.0 + nrm(shape, 0.02)

    def uni(shape, lo, hi):
        return jax.random.uniform(next(ks), shape, F32, lo, hi)

    return {
        "x": nrm((BATCH, SEQ, D_MODEL), 1.0),
        "ffn1_norm": gain((L, D_MODEL)),
        "ffn1_w_gate": nrm((L, D_MODEL, D_FF), D_MODEL ** -0.5),
        "ffn1_w_up": nrm((L, D_MODEL, D_FF), D_MODEL ** -0.5),
        "ffn1_w_down": nrm((L, D_FF, D_MODEL), D_FF ** -0.5),
        "mix_norm": gain((L, D_MODEL)),
        "w_in": nrm((L, D_MODEL, N_IN), D_MODEL ** -0.5),
        "w_out": nrm((L, D_MIX, D_MODEL), D_MIX ** -0.5),
        "rw_mu": uni((L, RW_COLS), 0.0, 1.0),
        "rw_w0": uni((L, RW_WIDTH), -2.0, 1.0),
        "rw_w_up": nrm((L, RW_DECAY_LORA, RW_WIDTH), RW_DECAY_LORA ** -0.5),
        "rw_a0": nrm((L, RW_WIDTH), 0.5),
        "rw_a_up": nrm((L, RW_AAA_LORA, RW_WIDTH), RW_AAA_LORA ** -0.5),
        "rw_g_up": nrm((L, RW_GATE_LORA, RW_WIDTH), RW_GATE_LORA ** -0.5),
        "rw_k_k": 0.85 + nrm((L, RW_WIDTH), 0.05),
        "rw_k_a": 1.0 + nrm((L, RW_WIDTH), 0.05),
        "rw_r_k": nrm((L, RW_HEADS, HEAD_DIM), 0.1),
        "rw_ln_w": gain((L, RW_WIDTH)),
        "rw_ln_b": nrm((L, RW_WIDTH), 0.01),
        "rw_vres_down": nrm((LV, D_MODEL, RW_MV_LORA), D_MODEL ** -0.5),
        "rw_vres_mu": uni((LV, RW_MV_LORA), 0.0, 1.0),
        "rw_v0": nrm((LV, RW_WIDTH), 0.5),
        "rw_vres_up": nrm((LV, RW_MV_LORA, RW_WIDTH), RW_MV_LORA ** -0.5),
        "hg_lb_logits": nrm((L, HG_FDIM), 0.5),
        "hg_norm": gain((L, HG_WIDTH)),
        "gl_conv": nrm((L, GL_CONV_WIDTH, GL_CONV_CH), GL_CONV_WIDTH ** -0.5),
        "gl_gate_up": nrm((L, GL_GATE_LORA, GL_KWIDTH), GL_GATE_LORA ** -0.5),
        "gl_gate_b": nrm((L, GL_KWIDTH), 0.1),
        "gl_norm": gain((L, GL_WIDTH)),
        "ffn2_norm": gain((L, D_MODEL)),
        "ffn2_w_gate": nrm((L, D_MODEL, D_FF), D_MODEL ** -0.5),
        "ffn2_w_up": nrm((L, D_MODEL, D_FF), D_MODEL ** -0.5),
        "ffn2_w_down": nrm((L, D_FF, D_MODEL), D_FF ** -0.5),
        "final_norm": gain((D_MODEL,)),
    }


def reference(x, ffn1_norm, ffn1_w_gate, ffn1_w_up, ffn1_w_down, mix_norm, w_in, w_out,
              rw_mu, rw_w0, rw_w_up, rw_a0, rw_a_up, rw_g_up, rw_k_k, rw_k_a, rw_r_k, rw_ln_w, rw_ln_b,
              rw_vres_down, rw_vres_mu, rw_v0, rw_vres_up, hg_lb_logits, hg_norm,
              gl_conv, gl_gate_up, gl_gate_b, gl_norm,
              ffn2_norm, ffn2_w_gate, ffn2_w_up, ffn2_w_down, final_norm):
    p = jax.nn.softmax(hg_lb_logits.astype(F32), axis=0)
    lower_bounds = jnp.cumsum(p, axis=0) - p[0]
    v_first = None
    for l in range(DEPTH):
        h = rms_norm(x, ffn1_norm[l])
        x = x + FFN_RESIDUAL_WEIGHT * swiglu(h, ffn1_w_gate[l], ffn1_w_up[l], ffn1_w_down[l])
        h = rms_norm(x, mix_norm[l])
        if l == 0:
            proj = h @ w_in[l]
            vres_z, vres_mu, v0, vres_up = None, None, None, None
        else:
            proj = h @ jnp.concatenate([w_in[l], rw_vres_down[l - 1]], axis=1)
            vres_z = proj[..., N_IN:]
            vres_mu, v0, vres_up = rw_vres_mu[l - 1], rw_v0[l - 1], rw_vres_up[l - 1]
        rw_z = proj[..., :RW_COLS]
        hg_z = proj[..., RW_COLS:RW_COLS + HG_COLS]
        gl_z = proj[..., RW_COLS + HG_COLS:N_IN]
        rw_o, v_first = rwkv7_mixer(rw_z, vres_z, v_first, rw_mu[l], rw_w0[l], rw_w_up[l], rw_a0[l],
                                    rw_a_up[l], rw_g_up[l], rw_k_k[l], rw_k_a[l], rw_r_k[l],
                                    rw_ln_w[l], rw_ln_b[l], vres_mu, v0, vres_up)
        hg_o = hgrn2_mixer(hg_z, lower_bounds[l], hg_norm[l])
        gl_o = gla_mixer(gl_z, gl_conv[l], gl_gate_up[l], gl_gate_b[l], gl_norm[l])
        mixed = jnp.concatenate([rw_o, hg_o, gl_o], axis=-1)
        x = x + (mixed @ w_out[l]).astype(x.dtype)
        h = rms_norm(x, ffn2_norm[l])
        x = x + FFN_RESIDUAL_WEIGHT * swiglu(h, ffn2_w_gate[l], ffn2_w_up[l], ffn2_w_down[l])
    return rms_norm(x, final_norm)
```

```python
import functools

import numpy as np
import jax
import jax.numpy as jnp
from jax import lax
from jax.experimental import pallas as pl
from jax.experimental.pallas import tpu as pltpu

F32 = jnp.float32
BF16 = jnp.bfloat16

D_MODEL = 1024
SEQ = 16384
DEPTH = 4
D_FF = 2816
NORM_EPS = 1e-5
L2_EPS = 1e-12
RW_GN_EPS = 64e-5
HG_GATE_FLOOR = 1e-30
GL_GATE_NORMALIZER = 16.0
GL_KEY_DIM = 48
GL_VAL_DIM = 96

LANE = 128
SUBLANE = 8
VMEM_LIMIT = 56 * 1024 * 1024

RW_OFF, RW_COLS = 0, 1408
HG_OFF, HG_COLS = 1408, 1536
GL_OFF = 2944
GL_QKV = 1024
GL_OG_OFF = GL_OFF + GL_QKV
MISC_OFF = GL_OG_OFF + 512
NP = MISC_OFF + LANE
D_MIXP = 384 + 256 + 512

TT = 128
RC = 64
N_LEVELS = 7
TM_FFN = 512
TM_PROJ = 512


def _mm(a, b):
    return jnp.dot(a.astype(BF16), b.astype(BF16), preferred_element_type=F32)


def _mm_nt(a, b):
    return lax.dot_general(a.astype(BF16), b.astype(BF16), (((1,), (1,)), ((), ())),
                           preferred_element_type=F32)


def _mm_tn(a, b):
    return lax.dot_general(a.astype(BF16), b.astype(BF16), (((0,), (0,)), ((), ())),
                           preferred_element_type=F32)


def _split3(x):
    hi = x.astype(BF16)
    r = x - hi.astype(F32)
    mid = r.astype(BF16)
    lo = (r - mid.astype(F32)).astype(BF16)
    return hi, mid, lo


def _sel_mm(c, x):
    hi, mid, lo = _split3(x)
    return (jnp.dot(c, hi, preferred_element_type=F32) + jnp.dot(c, mid, preferred_element_type=F32)
            + jnp.dot(c, lo, preferred_element_type=F32))


def _segsum(x, seg):
    hi, mid, lo = _split3(x)
    return (jnp.dot(hi, seg, preferred_element_type=F32) + jnp.dot(mid, seg, preferred_element_type=F32)
            + jnp.dot(lo, seg, preferred_element_type=F32))


def _sigmoid(x):
    return 1.0 / (1.0 + jnp.exp(-x))


def _silu(x):
    return x * _sigmoid(x)


def _softplus(x):
    return jnp.maximum(x, 0.0) + jnp.log1p(jnp.exp(-jnp.abs(x)))


def _rms(x, gain):
    return x * lax.rsqrt(jnp.mean(x * x, axis=-1, keepdims=True) + NORM_EPS) * gain


def _shift_rows(z, tail, k):
    rz = pltpu.roll(z, k, 0)
    rt = pltpu.roll(tail, k, 0)
    row = lax.broadcasted_iota(jnp.int32, (SUBLANE, z.shape[1]), 0)
    top = jnp.where(row < k, rt, rz[0:SUBLANE])
    return jnp.concatenate([top, rz[SUBLANE:]], axis=0)


def _const_spec(shape):
    nd = len(shape)
    return pl.BlockSpec(shape, lambda i, _nd=nd: (0,) * _nd, pipeline_mode=pl.Buffered(1))


def _ffn_body(*refs, final):
    if final:
        x_ref, g_ref, wg_ref, wu_ref, wd_ref, fg_ref, o_ref = refs
    else:
        x_ref, g_ref, wg_ref, wu_ref, wd_ref, o_ref = refs
    x = x_ref[...]
    h = _rms(x, g_ref[...]).astype(BF16)
    gate = jnp.dot(h, wg_ref[...], preferred_element_type=F32)
    up = jnp.dot(h, wu_ref[...], preferred_element_type=F32)
    act = (_silu(gate) * up).astype(BF16)
    y = x + 0.5 * jnp.dot(act, wd_ref[...], preferred_element_type=F32)
    if final:
        y = _rms(y, fg_ref[...])
    o_ref[...] = y


def _ffn(x, gain, wg, wu, wd, final_gain=None):
    t = x.shape[0]
    final = final_gain is not None
    in_specs = [
        pl.BlockSpec((TM_FFN, D_MODEL), lambda i: (i, 0)),
        _const_spec((1, D_MODEL)),
        _const_spec((D_MODEL, D_FF)),
        _const_spec((D_MODEL, D_FF)),
        _const_spec((D_FF, D_MODEL)),
    ]
    args = [x, gain, wg, wu, wd]
    if final:
        in_specs.append(_const_spec((1, D_MODEL)))
        args.append(final_gain)
    return pl.pallas_call(
        functools.partial(_ffn_body, final=final),
        grid=(t // TM_FFN,),
        in_specs=in_specs,
        out_specs=pl.BlockSpec((TM_FFN, D_MODEL), lambda i: (i, 0)),
        out_shape=jax.ShapeDtypeStruct((t, D_MODEL), F32),
        compiler_params=pltpu.CompilerParams(dimension_semantics=("arbitrary",),
                                             vmem_limit_bytes=VMEM_LIMIT),
        name="ffn_final" if final else "ffn",
    )(*args)


def _proj_body(x_ref, g_ref, w_ref, o_ref):
    h = _rms(x_ref[...], g_ref[...]).astype(BF16)
    o_ref[...] = jnp.dot(h, w_ref[...], preferred_element_type=F32)


def _proj(x, gain, w):
    t = x.shape[0]
    return pl.pallas_call(
        _proj_body,
        grid=(t // TM_PROJ,),
        in_specs=[pl.BlockSpec((TM_PROJ, D_MODEL), lambda i: (i, 0)),
                  _const_spec((1, D_MODEL)),
                  _const_spec((D_MODEL, NP))],
        out_specs=pl.BlockSpec((TM_PROJ, NP), lambda i: (i, 0)),
        out_shape=jax.ShapeDtypeStruct((t, NP), F32),
        compiler_params=pltpu.CompilerParams(dimension_semantics=("arbitrary",),
                                             vmem_limit_bytes=VMEM_LIMIT),
        name="proj",
    )(x, gain, w)


def _level_scores(q_t, k_t, e_all, lanes, lev):
    attn = jnp.zeros((TT, TT), F32)
    for lvl in range(N_LEVELS):
        e = e_all[lvl * TT:(lvl + 1) * TT, lanes]
        attn = jnp.where(lev == lvl, _mm_nt(q_t * e, k_t * e), attn)
    return jnp.where(lev == N_LEVELS, _mm_nt(q_t, k_t), attn)


def _mix_body(*refs, layer):
    first = layer == 0
    it = iter(refs)
    p_ref, x_ref = next(it), next(it)
    vf_ref = None if first else next(it)
    rwmu_ref, miscmu_ref, wup_ref, aup_ref, gup_ref = next(it), next(it), next(it), next(it), next(it)
    vup_ref = None if first else next(it)
    rwp_ref, lb_ref, hgn_ref = next(it), next(it), next(it)
    conv_ref, gateup_ref, gateb_ref, gln_ref, wout_ref = next(it), next(it), next(it), next(it), next(it)
    mstack_ref, lev_ref, seg64_ref, seg128_ref, tri_ref = next(it), next(it), next(it), next(it), next(it)
    o_ref = next(it)
    vfo_ref = next(it) if first else None
    tail_rw, tail_misc, tail_conv, st_rw, st_hg, st_gl = (next(it) for _ in range(6))

    @pl.when(pl.program_id(0) == 0)
    def _():
        tail_rw[...] = jnp.zeros_like(tail_rw)
        tail_misc[...] = jnp.zeros_like(tail_misc)
        tail_conv[...] = jnp.zeros_like(tail_conv)
        st_rw[...] = jnp.zeros_like(st_rw)
        st_hg[...] = jnp.zeros_like(st_hg)
        st_gl[...] = jnp.zeros_like(st_gl)

    lane = lax.broadcasted_iota(jnp.int32, (TT, LANE), 1)
    m_lo = lane < 64
    row = lax.broadcasted_iota(jnp.int32, (TT, TT), 0)
    col = lax.broadcasted_iota(jnp.int32, (TT, TT), 1)
    strict = row > col
    incl = row >= col
    eye = jnp.where(row == col, 1.0, 0.0).astype(F32)
    lev = lev_ref[...]
    seg64 = seg64_ref[...]
    mstack = mstack_ref[...]

    def halves(x):
        m = lax.broadcasted_iota(jnp.int32, x.shape, 1) < 64
        return jnp.concatenate([jnp.where(m, x, 0.0), jnp.where(m, 0.0, x)], axis=0)

    rw_z = p_ref[:, RW_OFF:RW_OFF + RW_COLS]
    prev = _shift_rows(rw_z, tail_rw[...], 1)
    tail_rw[...] = rw_z[TT - SUBLANE:TT]
    zs = rw_z + (prev - rw_z) * rwmu_ref[...]

    misc = p_ref[:, MISC_OFF:MISC_OFF + LANE]
    prev = _shift_rows(misc, tail_misc[...], 1)
    tail_misc[...] = misc[TT - SUBLANE:TT]
    misc = misc + (prev - misc) * miscmu_ref[...]

    r = zs[:, 0:384]
    k = zs[:, 384:768]
    v = zs[:, 768:1152]
    wa = zs[:, 1152:1280]
    g_dn = zs[:, 1280:1408]
    rwp = rwp_ref[...]
    w0, a0, k_k, k_a, r_k, ln_w, ln_b, v0 = (rwp[i:i + 1] for i in range(8))

    log_w = -jnp.exp(-_softplus(-(w0 + _mm(jnp.tanh(wa), wup_ref[...]))) - 0.5)
    a = _sigmoid(a0 + _mm(wa, aup_ref[...]))
    g_out = _mm(_sigmoid(g_dn), gup_ref[...])
    if first:
        vfo_ref[...] = v
    else:
        v = v + (vf_ref[...] - v) * _sigmoid(v0 + _mm(misc, vup_ref[...]))
    kk = k * k_k
    kk = kk / jnp.maximum(jnp.sqrt(_segsum(kk * kk, seg64)), L2_EPS)
    k = k * (1.0 + (a - 1.0) * k_a)
    alpha = kk * a

    g = _sel_mm(tri_ref[...], log_w)
    g_last = jnp.concatenate(
        [jnp.broadcast_to(g[(c + 1) * RC - 1:(c + 1) * RC], (RC, 384)) for c in range(TT // RC)], axis=0)
    inv = jnp.exp(-g)
    k_in, a_in = k * inv, alpha * inv
    beta_d = kk * jnp.exp(g - log_w)
    r_d = r * jnp.exp(g)
    dec = jnp.exp(g_last - g)
    k_end, a_end = k * dec, alpha * dec

    y_rows = []
    for c in range(TT // RC):
        rs = slice(c * RC, (c + 1) * RC)
        chunk_decay = jnp.exp(g[(c + 1) * RC - 1:(c + 1) * RC])
        y_cols = []
        for p in range(3):
            ls = slice(p * LANE, (p + 1) * LANE)
            bd, rd, kin, ain = halves(beta_d[rs, ls]), halves(r_d[rs, ls]), halves(k_in[rs, ls]), halves(a_in[rs, ls])
            kend, aend, vs = halves(k_end[rs, ls]), halves(a_end[rs, ls]), halves(v[rs, ls])
            sc = _mm_nt(jnp.concatenate([bd, rd], axis=0), jnp.concatenate([kin, ain], axis=0))
            a_bk = jnp.where(strict, sc[:TT, :TT], 0.0)
            a_ba = jnp.where(strict, sc[:TT, TT:], 0.0)
            a_rk = jnp.where(incl, sc[TT:, :TT], 0.0)
            a_ra = jnp.where(incl, sc[TT:, TT:], 0.0)
            pw = -a_ba
            t_inv = eye + pw
            for _ in range(5):
                pw = _mm(pw, pw)
                t_inv = t_inv + _mm(t_inv, pw)
            sol = _mm(t_inv, jnp.concatenate([bd, _mm(a_bk, vs)], axis=1))
            w_s, u_s = sol[:, :LANE], sol[:, LANE:]
            q_eff = rd - _mm(a_ra, w_s)
            o_s = _mm(jnp.concatenate([a_rk, -a_ra], axis=1), jnp.concatenate([vs, u_s], axis=0))
            lowrank = _mm_tn(aend, w_s)
            upd_t = _mm_tn(jnp.concatenate([vs, -u_s], axis=0), jnp.concatenate([kend, aend], axis=0))
            st = st_rw[p]
            o_s = o_s + _mm_nt(q_eff, st)
            y_cols.append(o_s[:RC] + o_s[RC:])
            st_rw[p] = st * chunk_decay[:, ls] - _mm_nt(st, lowrank) + upd_t
        y_rows.append(jnp.concatenate(y_cols, axis=1))
    y = jnp.concatenate(y_rows, axis=0)

    mean = _segsum(y, seg64) * (1.0 / 64)
    yc = y - mean
    var = _segsum(yc * yc, seg64) * (1.0 / 64)
    y = yc * lax.rsqrt(var + RW_GN_EPS) * ln_w + ln_b
    bonus = _segsum(r * k * r_k, seg64) * v
    rw_out = (y + bonus) * g_out

    hq = _silu(p_ref[:, HG_OFF:HG_OFF + 512])
    hf = p_ref[:, HG_OFF + 512:HG_OFF + 1024]
    hi_ = p_ref[:, HG_OFF + 1024:HG_OFF + 1280]
    hgate = p_ref[:, HG_OFF + 1280:HG_OFF + 1536]
    logits = lb_ref[...]
    ex = jnp.exp(logits - jnp.max(logits, axis=0, keepdims=True))
    prob = ex / jnp.sum(ex, axis=0, keepdims=True)
    lb = jnp.zeros((1, 512), F32)
    for i in range(1, layer + 1):
        lb = lb + prob[i:i + 1]
    forget = lb + (1.0 - lb) * _sigmoid(hf)
    log_f = jnp.log(jnp.maximum(forget, HG_GATE_FLOOR))
    hk = (1.0 - lb) * _sigmoid(-hf)
    e_all = jnp.exp(_sel_mm(mstack, log_f))
    e_incl = e_all[N_LEVELS * TT:(N_LEVELS + 1) * TT]
    e_end = e_all[(N_LEVELS + 1) * TT:(N_LEVELS + 2) * TT]
    bd_mask = (lax.broadcasted_iota(jnp.int32, (TT, 2 * LANE), 0) < 64) == (
        lax.broadcasted_iota(jnp.int32, (TT, 2 * LANE), 1) < LANE)
    hg_cols = []
    for pr in range(2):
        attn2 = jnp.concatenate(
            [_level_scores(hq[:, h * LANE:(h + 1) * LANE], hk[:, h * LANE:(h + 1) * LANE], e_all,
                           slice(h * LANE, (h + 1) * LANE), lev) for h in (2 * pr, 2 * pr + 1)], axis=1)
        ls2 = slice(pr * 2 * LANE, (pr + 1) * 2 * LANE)
        vt = hi_[:, pr * LANE:(pr + 1) * LANE]
        st = st_hg[pr]
        o = _mm(attn2, halves(vt)) + _mm_nt(hq[:, ls2] * e_incl[:, ls2], st)
        hg_cols.append(o)
        st_hg[pr] = st * e_incl[TT - 1:TT, ls2] + jnp.where(bd_mask, _mm_tn(vt, hk[:, ls2] * e_end[:, ls2]), 0.0)
    ho = jnp.concatenate(hg_cols, axis=1)
    ms = _segsum(ho * ho, seg64[:256, :256]) * (1.0 / 64)
    hg_out = ho * lax.rsqrt(ms + NORM_EPS) * hgn_ref[...] * _silu(hgate)

    cin = p_ref[:, GL_OFF:GL_OFF + GL_QKV]
    tail = tail_conv[...]
    cw = conv_ref[...]
    acc = cin * cw[3:4]
    for j in range(3):
        acc = acc + _shift_rows(cin, tail, 3 - j) * cw[j:j + 1]
    tail_conv[...] = cin[TT - SUBLANE:TT]
    qkv = _silu(acc)
    gq = qkv[:, 0:256] * (GL_KEY_DIM ** -0.5)
    gk = qkv[:, 256:512]
    gv = qkv[:, 512:1024]
    gate = _mm(misc, gateup_ref[...]) + gateb_ref[...]
    log_a = -_softplus(-gate) * (1.0 / GL_GATE_NORMALIZER)
    e_all = jnp.exp(_sel_mm(mstack, log_a))
    e_incl = e_all[N_LEVELS * TT:(N_LEVELS + 1) * TT]
    e_end = e_all[(N_LEVELS + 1) * TT:(N_LEVELS + 2) * TT]
    gl_cols = []
    for h in range(4):
        ls = slice((h // 2) * LANE, (h // 2 + 1) * LANE)
        m = m_lo if h % 2 == 0 else jnp.logical_not(m_lo)
        q_t = jnp.where(m, gq[:, ls], 0.0)
        k_t = gk[:, ls]
        v_h = gv[:, h * LANE:(h + 1) * LANE]
        attn = _level_scores(q_t, k_t, e_all, ls, lev)
        st = st_gl[h]
        gl_cols.append(_mm(attn, v_h) + _mm_nt(q_t * e_incl[:, ls], st))
        st_gl[h] = st * e_incl[TT - 1:TT, ls] + _mm_tn(v_h, jnp.where(m, k_t * e_end[:, ls], 0.0))
    go = jnp.concatenate(gl_cols, axis=1)
    ms = _segsum(go * go, seg128_ref[...]) * (1.0 / GL_VAL_DIM)
    og = p_ref[:, GL_OG_OFF:GL_OG_OFF + 512]
    gl_out = go * lax.rsqrt(ms + NORM_EPS) * gln_ref[...] * _silu(og)

    mixed = jnp.concatenate([rw_out, hg_out, gl_out], axis=1).astype(BF16)
    o_ref[...] = x_ref[...] + jnp.dot(mixed, wout_ref[...], preferred_element_type=F32)


def _mix(layer, proj, x, v_first, lw, consts):
    t = x.shape[0]
    first = layer == 0
    tile = lambda n: pl.BlockSpec((TT, n), lambda i: (i, 0))
    args, in_specs = [proj, x], [tile(NP), tile(D_MODEL)]
    if not first:
        args.append(v_first)
        in_specs.append(tile(384))
    names = ["rw_mu", "misc_mu", "w_up", "a_up", "g_up"] + ([] if first else ["vres_up"]) + [
        "rwp", "lb_logits", "hg_norm", "gl_conv", "gate_up", "gate_b", "gl_norm", "w_out"]
    for n in names:
        args.append(lw[n])
        in_specs.append(_const_spec(lw[n].shape))
    for n in ("mstack", "lev", "seg64", "seg128", "tri"):
        args.append(consts[n])
        in_specs.append(_const_spec(consts[n].shape))
    out_shape = [jax.ShapeDtypeStruct((t, D_MODEL), F32)]
    out_specs = [tile(D_MODEL)]
    if first:
        out_shape.append(jax.ShapeDtypeStruct((t, 384), F32))
        out_specs.append(tile(384))
    scratch = [
        pltpu.VMEM((SUBLANE, RW_COLS), F32), pltpu.VMEM((SUBLANE, LANE), F32), pltpu.VMEM((SUBLANE, GL_QKV), F32),
        pltpu.VMEM((3, LANE, LANE), F32), pltpu.VMEM((2, LANE, 2 * LANE), F32), pltpu.VMEM((4, LANE, LANE), F32),
    ]
    res = pl.pallas_call(
        functools.partial(_mix_body, layer=layer),
        grid=(t // TT,),
        in_specs=in_specs,
        out_specs=out_specs,
        out_shape=out_shape,
        scratch_shapes=scratch,
        compiler_params=pltpu.CompilerParams(dimension_semantics=("arbitrary",),
                                             vmem_limit_bytes=VMEM_LIMIT),
        name="mixer0" if first else "mixer",
    )(*args)
    return (res[0], res[1]) if first else (res[0], v_first)


def _level_constants():
    c = TT
    mats = []
    b = c // 2
    while b >= 1:
        m = np.zeros((c, c), np.float32)
        for s in range(c):
            mid = (s // (2 * b)) * 2 * b + b
            if s >= mid:
                m[s, mid + 1:s + 1] = 1.0
            else:
                m[s, s + 1:mid + 1] = 1.0
        mats.append(m)
        b //= 2
    mats.append(np.tril(np.ones((c, c), np.float32)))
    mats.append(np.triu(np.ones((c, c), np.float32), 1))
    s = np.arange(c)[:, None]
    j = np.arange(c)[None, :]
    x = s ^ j
    hb = np.where(x > 0, np.floor(np.log2(np.maximum(x, 1))).astype(np.int64), 0)
    lev = np.where(s > j, (N_LEVELS - 1) - hb, np.where(s == j, N_LEVELS, -1)).astype(np.int32)
    seg = lambda n, w: np.kron(np.eye(n // w, dtype=np.float32), np.ones((w, w), np.float32))
    tri = np.kron(np.eye(TT // RC, dtype=np.float32), np.tril(np.ones((RC, RC), np.float32)))
    return {
        "mstack": jnp.asarray(np.concatenate(mats, axis=0), BF16),
        "lev": jnp.asarray(lev),
        "seg64": jnp.asarray(seg(384, 64), BF16),
        "seg128": jnp.asarray(seg(512, 128), BF16),
        "tri": jnp.asarray(tri, BF16),
    }


def _pad_heads(w, heads, dim, to):
    w = w.reshape(w.shape[:-1] + (heads, dim))
    w = jnp.pad(w, [(0, 0)] * (w.ndim - 1) + [(0, to - dim)])
    return w.reshape(w.shape[:-2] + (heads * to,))


def _layer_weights(l, w_in, w_out, rw_mu, rw_w0, rw_w_up, rw_a0, rw_a_up, rw_g_up, rw_k_k, rw_k_a, rw_r_k,
                   rw_ln_w, rw_ln_b, rw_vres_down, rw_vres_mu, rw_v0, rw_vres_up, hg_lb_logits, hg_norm,
                   gl_conv, gl_gate_up, gl_gate_b, gl_norm):
    wi = w_in[l]
    gl = wi[:, RW_COLS + HG_COLS:]
    q, k, v, gate_dn, og = gl[:, 0:192], gl[:, 192:384], gl[:, 384:768], gl[:, 768:784], gl[:, 784:1168]
    if l == 0:
        vres = jnp.zeros((D_MODEL, 32), F32)
        vres_mu = jnp.zeros((32,), F32)
    else:
        vres = rw_vres_down[l - 1]
        vres_mu = rw_vres_mu[l - 1]
    misc = jnp.concatenate([gate_dn, vres, jnp.zeros((D_MODEL, LANE - 48), F32)], axis=1)
    w_in_p = jnp.concatenate([wi[:, :RW_COLS + HG_COLS], _pad_heads(q, 4, 48, 64), _pad_heads(k, 4, 48, 64),
                              _pad_heads(v, 4, 96, 128), _pad_heads(og, 4, 96, 128), misc], axis=1).astype(BF16)
    wo = w_out[l]
    wo_gl = jnp.pad(wo[640:].reshape(4, 96, D_MODEL), ((0, 0), (0, 32), (0, 0))).reshape(512, D_MODEL)
    w_out_p = jnp.concatenate([wo[:640], wo_gl], axis=0).astype(BF16)
    zeros64 = jnp.zeros((64, 384), F32)
    conv = gl_conv[l]
    conv_p = jnp.concatenate([_pad_heads(conv[:, 0:192], 4, 48, 64), _pad_heads(conv[:, 192:384], 4, 48, 64),
                              _pad_heads(conv[:, 384:768], 4, 96, 128)], axis=1)
    lw = {
        "w_in": w_in_p,
        "w_out": w_out_p,
        "rw_mu": rw_mu[l][None],
        "misc_mu": jnp.concatenate([jnp.zeros((16,), F32), vres_mu, jnp.zeros((LANE - 48,), F32)])[None],
        "w_up": jnp.concatenate([rw_w_up[l], zeros64], axis=0).astype(BF16),
        "a_up": jnp.concatenate([zeros64, rw_a_up[l]], axis=0).astype(BF16),
        "g_up": rw_g_up[l].astype(BF16),
        "rwp": jnp.stack([rw_w0[l], rw_a0[l], rw_k_k[l], rw_k_a[l], rw_r_k[l].reshape(384), rw_ln_w[l], rw_ln_b[l],
                          rw_v0[l - 1] if l > 0 else jnp.zeros((384,), F32)], axis=0),
        "lb_logits": hg_lb_logits,
        "hg_norm": hg_norm[l][None],
        "gl_conv": conv_p,
        "gate_up": jnp.pad(_pad_heads(gl_gate_up[l], 4, 48, 64), ((0, LANE - 16), (0, 0))).astype(BF16),
        "gate_b": _pad_heads(gl_gate_b[l], 4, 48, 64)[None],
        "gl_norm": _pad_heads(gl_norm[l], 4, 96, 128)[None],
    }
    if l > 0:
        lw["vres_up"] = jnp.pad(rw_vres_up[l - 1], ((16, LANE - 48), (0, 0))).astype(BF16)
    return lw


def kernel(x, ffn1_norm, ffn1_w_gate, ffn1_w_up, ffn1_w_down, mix_norm, w_in, w_out, rw_mu, rw_w0, rw_w_up, rw_a0, rw_a_up, rw_g_up, rw_k_k, rw_k_a, rw_r_k, rw_ln_w, rw_ln_b, rw_vres_down, rw_vres_mu, rw_v0, rw_vres_up, hg_lb_logits, hg_norm, gl_conv, gl_gate_up, gl_gate_b, gl_norm, ffn2_norm, ffn2_w_gate, ffn2_w_up, ffn2_w_down, final_norm):
    bsz, seq, d = x.shape
    assert bsz == 1 and d == D_MODEL and seq % TM_FFN == 0
    consts = _level_constants()
    h = x.reshape(seq, d)
    v_first = None
    for l in range(DEPTH):
        lw = _layer_weights(l, w_in, w_out, rw_mu, rw_w0, rw_w_up, rw_a0, rw_a_up, rw_g_up, rw_k_k, rw_k_a, rw_r_k,
                            rw_ln_w, rw_ln_b, rw_vres_down, rw_vres_mu, rw_v0, rw_vres_up, hg_lb_logits, hg_norm,
                            gl_conv, gl_gate_up, gl_gate_b, gl_norm)
        h = _ffn(h, ffn1_norm[l][None], ffn1_w_gate[l].astype(BF16), ffn1_w_up[l].astype(BF16),
                 ffn1_w_down[l].astype(BF16))
        proj = _proj(h, mix_norm[l][None], lw["w_in"])
        h, v_first = _mix(l, proj, h, v_first, lw, consts)
        h = _ffn(h, ffn2_norm[l][None], ffn2_w_gate[l].astype(BF16), ffn2_w_up[l].astype(BF16),
                 ffn2_w_down[l].astype(BF16), final_gain=final_norm[None] if l == DEPTH - 1 else None)
    return h.reshape(bsz, seq, d)
```

```python
import functools

import numpy as np
import jax
import jax.numpy as jnp
from jax import lax
from jax.experimental import pallas as pl
from jax.experimental.pallas import tpu as pltpu

F32 = jnp.float32
BF16 = jnp.bfloat16

D_MODEL = 1024
DEPTH = 4
D_FF = 2816
NORM_EPS = 1e-5
L2_EPS = 1e-12
RW_GN_EPS = 64e-5
HG_GATE_FLOOR = 1e-30
GL_GATE_NORMALIZER = 16.0
GL_KEY_DIM = 48
GL_VAL_DIM = 96

LANE = 128
SUBLANE = 8
VMEM_LIMIT = 56 * 1024 * 1024

RW_OFF, RW_COLS = 0, 1408
HG_OFF, HG_COLS = 1408, 1536
GL_OFF = 2944
GL_QKV = 1024
GL_OG_OFF = GL_OFF + GL_QKV
MISC_OFF = GL_OG_OFF + 512
NP = MISC_OFF + LANE
RW_W = 384

TT = 256
GC = 128
RC = 64
N_LEVELS = 7
TM_FFN = 512
TM_PROJ = 512


def _mm(a, b):
    return jnp.dot(a.astype(BF16), b.astype(BF16), preferred_element_type=F32)


def _mm_nt(a, b):
    return lax.dot_general(a.astype(BF16), b.astype(BF16), (((1,), (1,)), ((), ())),
                           preferred_element_type=F32)


def _mm_tn(a, b):
    return lax.dot_general(a.astype(BF16), b.astype(BF16), (((0,), (0,)), ((), ())),
                           preferred_element_type=F32)


def _split3(x):
    hi = x.astype(BF16)
    r = x - hi.astype(F32)
    mid = r.astype(BF16)
    lo = (r - mid.astype(F32)).astype(BF16)
    return hi, mid, lo


def _sel_mm(c, x):
    hi, mid, lo = _split3(x)
    return (jnp.dot(c, hi, preferred_element_type=F32) + jnp.dot(c, mid, preferred_element_type=F32)
            + jnp.dot(c, lo, preferred_element_type=F32))


def _segsum(x, seg):
    hi, mid, lo = _split3(x)
    return (jnp.dot(hi, seg, preferred_element_type=F32) + jnp.dot(mid, seg, preferred_element_type=F32)
            + jnp.dot(lo, seg, preferred_element_type=F32))


def _sigmoid(x):
    return 1.0 / (1.0 + jnp.exp(-x))


def _silu(x):
    return x * _sigmoid(x)


def _softplus(x):
    return jnp.maximum(x, 0.0) + jnp.log1p(jnp.exp(-jnp.abs(x)))


def _rms(x, gain):
    return x * lax.rsqrt(jnp.mean(x * x, axis=-1, keepdims=True) + NORM_EPS) * gain


def _shift_rows(z, tail, k):
    rz = pltpu.roll(z, k, 0)
    rt = pltpu.roll(tail, k, 0)
    row = lax.broadcasted_iota(jnp.int32, (SUBLANE, z.shape[1]), 0)
    top = jnp.where(row < k, rt, rz[0:SUBLANE])
    return jnp.concatenate([top, rz[SUBLANE:]], axis=0)


def _halves(x):
    m = lax.broadcasted_iota(jnp.int32, x.shape, 1) < 64
    return jnp.concatenate([jnp.where(m, x, 0.0), jnp.where(m, 0.0, x)], axis=0)


def _interleave(*seqs):
    n = max(len(s) for s in seqs)
    for i in range(n):
        for s in seqs:
            for j in range(i * len(s) // n, (i + 1) * len(s) // n):
                s[j]()


def _const_spec(shape):
    nd = len(shape)
    return pl.BlockSpec(shape, lambda i, _nd=nd: (0,) * _nd, pipeline_mode=pl.Buffered(1))


def _ffn_body(*refs, final):
    if final:
        x_ref, g_ref, wg_ref, wu_ref, wd_ref, fg_ref, o_ref = refs
    else:
        x_ref, g_ref, wg_ref, wu_ref, wd_ref, o_ref = refs
    x = x_ref[...]
    h = _rms(x, g_ref[...]).astype(BF16)
    gate = jnp.dot(h, wg_ref[...], preferred_element_type=F32)
    up = jnp.dot(h, wu_ref[...], preferred_element_type=F32)
    act = (_silu(gate) * up).astype(BF16)
    y = x + 0.5 * jnp.dot(act, wd_ref[...], preferred_element_type=F32)
    if final:
        y = _rms(y, fg_ref[...])
    o_ref[...] = y


def _ffn(x, gain, wg, wu, wd, final_gain=None):
    t = x.shape[0]
    final = final_gain is not None
    in_specs = [
        pl.BlockSpec((TM_FFN, D_MODEL), lambda i: (i, 0)),
        _const_spec((1, D_MODEL)),
        _const_spec((D_MODEL, D_FF)),
        _const_spec((D_MODEL, D_FF)),
        _const_spec((D_FF, D_MODEL)),
    ]
    args = [x, gain, wg, wu, wd]
    if final:
        in_specs.append(_const_spec((1, D_MODEL)))
        args.append(final_gain)
    return pl.pallas_call(
        functools.partial(_ffn_body, final=final),
        grid=(t // TM_FFN,),
        in_specs=in_specs,
        out_specs=pl.BlockSpec((TM_FFN, D_MODEL), lambda i: (i, 0)),
        out_shape=jax.ShapeDtypeStruct((t, D_MODEL), F32),
        compiler_params=pltpu.CompilerParams(dimension_semantics=("arbitrary",),
                                             vmem_limit_bytes=VMEM_LIMIT),
        name="ffn_final" if final else "ffn",
    )(*args)


def _proj_body(x_ref, g_ref, w_ref, o_ref):
    h = _rms(x_ref[...], g_ref[...]).astype(BF16)
    o_ref[...] = jnp.dot(h, w_ref[...], preferred_element_type=F32)


def _proj(x, gain, w):
    t = x.shape[0]
    return pl.pallas_call(
        _proj_body,
        grid=(t // TM_PROJ,),
        in_specs=[pl.BlockSpec((TM_PROJ, D_MODEL), lambda i: (i, 0)),
                  _const_spec((1, D_MODEL)),
                  _const_spec((D_MODEL, NP))],
        out_specs=pl.BlockSpec((TM_PROJ, NP), lambda i: (i, 0)),
        out_shape=jax.ShapeDtypeStruct((t, NP), F32),
        compiler_params=pltpu.CompilerParams(dimension_semantics=("arbitrary",),
                                             vmem_limit_bytes=VMEM_LIMIT),
        name="proj",
    )(x, gain, w)


def _decay_levels(log_d, g):
    kdim = log_d.shape[1]
    levels = []
    b = GC // 2
    while b >= 4:
        mids = [jnp.broadcast_to(g[m:m + 1], (2 * b, kdim)) for m in range(b, GC, 2 * b)]
        g_mid = mids[0] if len(mids) == 1 else jnp.concatenate(mids, axis=0)
        levels.append(jnp.exp(-jnp.abs(g - g_mid)))
        b //= 2
    up1 = pltpu.roll(log_d, GC - 1, 0)
    up2 = pltpu.roll(log_d, GC - 2, 0)
    r = lax.broadcasted_iota(jnp.int32, log_d.shape, 0)
    r4 = r & 3
    levels.append(jnp.exp(jnp.where(r4 == 0, up1 + up2, jnp.where(r4 == 1, up1, jnp.where(r4 == 2, 0.0, log_d)))))
    levels.append(jnp.exp(jnp.where((r & 1) == 0, up1, 0.0)))
    return levels, jnp.exp(g), jnp.exp(g[GC - 1:GC] - g)


def _mix_body(*refs, layer):
    first = layer == 0
    it = iter(refs)
    p_ref, x_ref = next(it), next(it)
    vf_ref = None if first else next(it)
    rwmu_ref, miscmu_ref, wup_ref, aup_ref, gup_ref = next(it), next(it), next(it), next(it), next(it)
    vup_ref = None if first else next(it)
    rwp_ref, lb_ref, hgn_ref = next(it), next(it), next(it)
    conv_ref, gateup_ref, gateb_ref, gln_ref, wout_ref = next(it), next(it), next(it), next(it), next(it)
    lev_ref, seg64_ref, seg128_ref, tri64_ref, tri128_ref = next(it), next(it), next(it), next(it), next(it)
    o_ref = next(it)
    vfo_ref = next(it) if first else None
    tail_rw, tail_misc, tail_conv, st_rw, st_hg, st_gl = (next(it) for _ in range(6))

    @pl.when(pl.program_id(0) == 0)
    def _():
        tail_rw[...] = jnp.zeros_like(tail_rw)
        tail_misc[...] = jnp.zeros_like(tail_misc)
        tail_conv[...] = jnp.zeros_like(tail_conv)
        st_rw[...] = jnp.zeros_like(st_rw)
        st_hg[...] = jnp.zeros_like(st_hg)
        st_gl[...] = jnp.zeros_like(st_gl)

    m_lo = lax.broadcasted_iota(jnp.int32, (GC, LANE), 1) < 64
    row = lax.broadcasted_iota(jnp.int32, (GC, GC), 0)
    col = lax.broadcasted_iota(jnp.int32, (GC, GC), 1)
    strict = row > col
    incl = row >= col
    eye = jnp.where(row == col, 1.0, 0.0).astype(F32)
    lev = lev_ref[...]
    seg64 = seg64_ref[...]
    n_rc = TT // RC
    n_gc = TT // GC

    rw_z = p_ref[:, RW_OFF:RW_OFF + RW_COLS]
    prev = _shift_rows(rw_z, tail_rw[...], 1)
    tail_rw[...] = rw_z[TT - SUBLANE:TT]
    zs = rw_z + (prev - rw_z) * rwmu_ref[...]

    misc = p_ref[:, MISC_OFF:MISC_OFF + LANE]
    prev = _shift_rows(misc, tail_misc[...], 1)
    tail_misc[...] = misc[TT - SUBLANE:TT]
    misc = misc + (prev - misc) * miscmu_ref[...]

    r = zs[:, 0:384]
    k = zs[:, 384:768]
    v = zs[:, 768:1152]
    wa = zs[:, 1152:1280]
    g_dn = zs[:, 1280:1408]
    rwp = rwp_ref[...]
    w0, a0, k_k, k_a, r_k, ln_w, ln_b, v0 = (rwp[i:i + 1] for i in range(8))

    log_w = -jnp.exp(-_softplus(-(w0 + _mm(jnp.tanh(wa), wup_ref[...]))) - 0.5)
    a = _sigmoid(a0 + _mm(wa, aup_ref[...]))
    g_out = _mm(_sigmoid(g_dn), gup_ref[...])
    if first:
        vfo_ref[...] = v
    else:
        v = v + (vf_ref[...] - v) * _sigmoid(v0 + _mm(misc, vup_ref[...]))
    kk = k * k_k
    kk = kk / jnp.maximum(jnp.sqrt(_segsum(kk * kk, seg64)), L2_EPS)
    k = k * (1.0 + (a - 1.0) * k_a)
    alpha = kk * a

    tri64 = tri64_ref[...]
    g = jnp.concatenate([_sel_mm(tri64, log_w[c * GC:(c + 1) * GC]) for c in range(n_gc)], axis=0)
    g_last = jnp.concatenate(
        [jnp.broadcast_to(g[(c + 1) * RC - 1:(c + 1) * RC], (RC, RW_W)) for c in range(n_rc)], axis=0)
    inv = jnp.exp(-g)
    k_in, a_in = k * inv, alpha * inv
    beta_d = kk * jnp.exp(g - log_w)
    r_d = r * jnp.exp(g)
    dec = jnp.exp(g_last - g)
    k_end, a_end = k * dec, alpha * dec

    hq = _silu(p_ref[:, HG_OFF:HG_OFF + 512])
    hf = p_ref[:, HG_OFF + 512:HG_OFF + 1024]
    hv = p_ref[:, HG_OFF + 1024:HG_OFF + 1280]
    logits = lb_ref[...]
    ex = jnp.exp(logits - jnp.max(logits, axis=0, keepdims=True))
    prob = ex / jnp.sum(ex, axis=0, keepdims=True)
    lb = jnp.zeros((1, 512), F32)
    for i in range(1, layer + 1):
        lb = lb + prob[i:i + 1]
    log_f = jnp.log(jnp.maximum(lb + (1.0 - lb) * _sigmoid(hf), HG_GATE_FLOOR))
    hk = (1.0 - lb) * _sigmoid(-hf)

    cin = p_ref[:, GL_OFF:GL_OFF + GL_QKV]
    tail = tail_conv[...]
    cw = conv_ref[...]
    acc = cin * cw[3:4]
    for j in range(3):
        acc = acc + _shift_rows(cin, tail, 3 - j) * cw[j:j + 1]
    tail_conv[...] = cin[TT - SUBLANE:TT]
    qkv = _silu(acc)
    gq = qkv[:, 0:256] * (GL_KEY_DIM ** -0.5)
    gk = qkv[:, 256:512]
    gv = qkv[:, 512:1024]
    log_a = -_softplus(-(_mm(misc, gateup_ref[...]) + gateb_ref[...])) * (1.0 / GL_GATE_NORMALIZER)

    tri128 = tri128_ref[...]

    probs = [(c, p) for c in range(n_rc) for p in range(3)]
    cut = lambda x, c, p: _halves(x[c * RC:(c + 1) * RC, p * LANE:(p + 1) * LANE])
    bd = [cut(beta_d, c, p) for c, p in probs]
    rd = [cut(r_d, c, p) for c, p in probs]
    vs = [cut(v, c, p) for c, p in probs]
    sc = [_mm_nt(jnp.concatenate([bd[i], rd[i]], axis=0),
                 jnp.concatenate([cut(k_in, c, p), cut(a_in, c, p)], axis=0)) for i, (c, p) in enumerate(probs)]
    a_bk = [jnp.where(strict, s[:GC, :GC], 0.0) for s in sc]
    a_rk = [jnp.where(incl, s[GC:, :GC], 0.0) for s in sc]
    a_ra = [jnp.where(incl, s[GC:, GC:], 0.0) for s in sc]
    pw = [jnp.where(strict, -s[:GC, GC:], 0.0) for s in sc]
    t_inv = [eye + x for x in pw]
    for _ in range(5):
        pw = [_mm(x, x) for x in pw]
        t_inv = [t + _mm(t, x) for t, x in zip(t_inv, pw)]
    abv = [_mm(a_bk[i], vs[i]) for i in range(len(probs))]
    sol = [_mm(t_inv[i], jnp.concatenate([bd[i], abv[i]], axis=1)) for i in range(len(probs))]
    w_s = [s[:, :LANE] for s in sol]
    u_s = [s[:, LANE:] for s in sol]
    q_eff = [rd[i] - _mm(a_ra[i], w_s[i]) for i in range(len(probs))]
    o_rw = [_mm(jnp.concatenate([a_rk[i], -a_ra[i]], axis=1), jnp.concatenate([vs[i], u_s[i]], axis=0))
            for i in range(len(probs))]
    aend = [cut(a_end, c, p) for c, p in probs]
    lowrank = [_mm_tn(aend[i], w_s[i]) for i in range(len(probs))]
    upd_t = [_mm_tn(jnp.concatenate([vs[i], -u_s[i]], axis=0),
                    jnp.concatenate([cut(k_end, c, p), aend[i]], axis=0)) for i, (c, p) in enumerate(probs)]

    y_tiles = {}
    st_val = [st_rw[p] for p in range(3)]

    def rw_step(c):
        def run():
            chunk_decay = jnp.exp(g[(c + 1) * RC - 1:(c + 1) * RC])
            for p in range(3):
                i = c * 3 + p
                st = st_val[p]
                o_s = o_rw[i] + _mm_nt(q_eff[i], st)
                y_tiles[(c, p)] = o_s[:RC] + o_s[RC:]
                st_val[p] = st * chunk_decay[:, p * LANE:(p + 1) * LANE] - _mm_nt(st, lowrank[i]) + upd_t[i]
        return run

    hg_attn, gl_attn, hg_e, gl_e = {}, {}, {}, {}

    def level_scores(q_t, k_t, levels, lanes):
        attn = jnp.zeros((GC, GC), F32)
        for lvl, e in enumerate(levels):
            e = e[:, lanes]
            attn = jnp.where(lev == lvl, _mm_nt(q_t * e, k_t * e), attn)
        return jnp.where(lev == N_LEVELS, _mm_nt(q_t, k_t), attn)

    def hg_scores(c, h):
        def run():
            rs = slice(c * GC, (c + 1) * GC)
            if h == 0:
                lf = log_f[rs]
                hg_e[c] = _decay_levels(lf, _sel_mm(tri128, lf))
            ls = slice(h * LANE, (h + 1) * LANE)
            hg_attn[(c, h)] = level_scores(hq[rs, ls], hk[rs, ls], hg_e[c][0], ls)
        return run

    def gl_scores(c, h):
        def run():
            rs = slice(c * GC, (c + 1) * GC)
            if h == 0:
                la = log_a[rs]
                gl_e[c] = _decay_levels(la, _sel_mm(tri128, la))
            ls = slice((h // 2) * LANE, (h // 2 + 1) * LANE)
            m = m_lo if h % 2 == 0 else jnp.logical_not(m_lo)
            gl_attn[(c, h)] = level_scores(jnp.where(m, gq[rs, ls], 0.0), gk[rs, ls], gl_e[c][0], ls)
        return run

    _interleave([rw_step(c) for c in range(n_rc)],
                [hg_scores(c, h) for c in range(n_gc) for h in range(4)],
                [gl_scores(c, h) for c in range(n_gc) for h in range(4)])
    for p in range(3):
        st_rw[p] = st_val[p]
    y = jnp.concatenate([jnp.concatenate([y_tiles[(c, p)] for p in range(3)], axis=1) for c in range(n_rc)], axis=0)

    mean = _segsum(y, seg64) * (1.0 / 64)
    yc = y - mean
    var = _segsum(yc * yc, seg64) * (1.0 / 64)
    y = yc * lax.rsqrt(var + RW_GN_EPS) * ln_w + ln_b
    bonus = _segsum(r * k * r_k, seg64) * v
    rw_out = (y + bonus) * g_out

    bd_mask = (lax.broadcasted_iota(jnp.int32, (GC, 2 * LANE), 0) < 64) == (
        lax.broadcasted_iota(jnp.int32, (GC, 2 * LANE), 1) < LANE)
    hg_st = [st_hg[pr] for pr in range(2)]
    gl_st = [st_gl[h] for h in range(4)]
    hg_rows, gl_rows = [], []
    for c in range(n_gc):
        rs = slice(c * GC, (c + 1) * GC)
        _, he_incl, he_end = hg_e[c]
        _, ge_incl, ge_end = gl_e[c]
        cols = []
        for pr in range(2):
            ls2 = slice(pr * 2 * LANE, (pr + 1) * 2 * LANE)
            attn2 = jnp.concatenate([hg_attn[(c, 2 * pr)], hg_attn[(c, 2 * pr + 1)]], axis=1)
            vt = hv[rs, pr * LANE:(pr + 1) * LANE]
            cols.append(_mm(attn2, _halves(vt)) + _mm_nt(hq[rs, ls2] * he_incl[:, ls2], hg_st[pr]))
        for h in range(4):
            ls = slice((h // 2) * LANE, (h // 2 + 1) * LANE)
            m = m_lo if h % 2 == 0 else jnp.logical_not(m_lo)
            cols.append(_mm(gl_attn[(c, h)], gv[rs, h * LANE:(h + 1) * LANE])
                        + _mm_nt(jnp.where(m, gq[rs, ls] * ge_incl[:, ls], 0.0), gl_st[h]))
        hg_rows.append(jnp.concatenate(cols[:2], axis=1))
        gl_rows.append(jnp.concatenate(cols[2:], axis=1))
        for pr in range(2):
            ls2 = slice(pr * 2 * LANE, (pr + 1) * 2 * LANE)
            vt = hv[rs, pr * LANE:(pr + 1) * LANE]
            hg_st[pr] = hg_st[pr] * he_incl[GC - 1:GC, ls2] + jnp.where(
                bd_mask, _mm_tn(vt, hk[rs, ls2] * he_end[:, ls2]), 0.0)
        for h in range(4):
            ls = slice((h // 2) * LANE, (h // 2 + 1) * LANE)
            m = m_lo if h % 2 == 0 else jnp.logical_not(m_lo)
            gl_st[h] = gl_st[h] * ge_incl[GC - 1:GC, ls] + _mm_tn(
                gv[rs, h * LANE:(h + 1) * LANE], jnp.where(m, gk[rs, ls] * ge_end[:, ls], 0.0))
    for pr in range(2):
        st_hg[pr] = hg_st[pr]
    for h in range(4):
        st_gl[h] = gl_st[h]
    ho = jnp.concatenate(hg_rows, axis=0)
    go = jnp.concatenate(gl_rows, axis=0)

    ms = _segsum(ho * ho, seg64[:256, :256]) * (1.0 / 64)
    hg_out = ho * lax.rsqrt(ms + NORM_EPS) * hgn_ref[...] * _silu(p_ref[:, HG_OFF + 1280:HG_OFF + 1536])
    ms = _segsum(go * go, seg128_ref[...]) * (1.0 / GL_VAL_DIM)
    gl_out = go * lax.rsqrt(ms + NORM_EPS) * gln_ref[...] * _silu(p_ref[:, GL_OG_OFF:GL_OG_OFF + 512])

    mixed = jnp.concatenate([rw_out, hg_out, gl_out], axis=1).astype(BF16)
    o_ref[...] = x_ref[...] + jnp.dot(mixed, wout_ref[...], preferred_element_type=F32)


def _mix(layer, proj, x, v_first, lw, consts):
    t = x.shape[0]
    first = layer == 0
    tile = lambda n: pl.BlockSpec((TT, n), lambda i: (i, 0))
    args, in_specs = [proj, x], [tile(NP), tile(D_MODEL)]
    if not first:
        args.append(v_first)
        in_specs.append(tile(RW_W))
    names = ["rw_mu", "misc_mu", "w_up", "a_up", "g_up"] + ([] if first else ["vres_up"]) + [
        "rwp", "lb_logits", "hg_norm", "gl_conv", "gate_up", "gate_b", "gl_norm", "w_out"]
    for n in names:
        args.append(lw[n])
        in_specs.append(_const_spec(lw[n].shape))
    for n in ("lev", "seg64", "seg128", "tri64", "tri128"):
        args.append(consts[n])
        in_specs.append(_const_spec(consts[n].shape))
    out_shape = [jax.ShapeDtypeStruct((t, D_MODEL), F32)]
    out_specs = [tile(D_MODEL)]
    if first:
        out_shape.append(jax.ShapeDtypeStruct((t, RW_W), F32))
        out_specs.append(tile(RW_W))
    scratch = [
        pltpu.VMEM((SUBLANE, RW_COLS), F32), pltpu.VMEM((SUBLANE, LANE), F32), pltpu.VMEM((SUBLANE, GL_QKV), F32),
        pltpu.VMEM((3, LANE, LANE), F32), pltpu.VMEM((2, LANE, 2 * LANE), F32), pltpu.VMEM((4, LANE, LANE), F32),
    ]
    res = pl.pallas_call(
        functools.partial(_mix_body, layer=layer),
        grid=(t // TT,),
        in_specs=in_specs,
        out_specs=out_specs,
        out_shape=out_shape,
        scratch_shapes=scratch,
        compiler_params=pltpu.CompilerParams(dimension_semantics=("arbitrary",),
                                             vmem_limit_bytes=VMEM_LIMIT),
        name="mixer0" if first else "mixer",
    )(*args)
    return (res[0], res[1]) if first else (res[0], v_first)


def _mixer_constants():
    s = np.arange(GC)[:, None]
    j = np.arange(GC)[None, :]
    x = s ^ j
    hb = np.where(x > 0, np.floor(np.log2(np.maximum(x, 1))).astype(np.int64), 0)
    lev = np.where(s > j, (N_LEVELS - 1) - hb, np.where(s == j, N_LEVELS, -1)).astype(np.int32)
    seg = lambda n, w: np.kron(np.eye(n // w, dtype=np.float32), np.ones((w, w), np.float32))
    tri64 = np.kron(np.eye(GC // RC, dtype=np.float32), np.tril(np.ones((RC, RC), np.float32)))
    return {
        "lev": jnp.asarray(lev),
        "seg64": jnp.asarray(seg(RW_W, 64), BF16),
        "seg128": jnp.asarray(seg(512, 128), BF16),
        "tri64": jnp.asarray(tri64, BF16),
        "tri128": jnp.asarray(np.tril(np.ones((GC, GC), np.float32)), BF16),
    }


def _pad_heads(w, heads, dim, to):
    w = w.reshape(w.shape[:-1] + (heads, dim))
    w = jnp.pad(w, [(0, 0)] * (w.ndim - 1) + [(0, to - dim)])
    return w.reshape(w.shape[:-2] + (heads * to,))


def _layer_weights(l, w_in, w_out, rw_mu, rw_w0, rw_w_up, rw_a0, rw_a_up, rw_g_up, rw_k_k, rw_k_a, rw_r_k,
                   rw_ln_w, rw_ln_b, rw_vres_down, rw_vres_mu, rw_v0, rw_vres_up, hg_lb_logits, hg_norm,
                   gl_conv, gl_gate_up, gl_gate_b, gl_norm):
    wi = w_in[l]
    gl = wi[:, RW_COLS + HG_COLS:]
    q, k, v, gate_dn, og = gl[:, 0:192], gl[:, 192:384], gl[:, 384:768], gl[:, 768:784], gl[:, 784:1168]
    if l == 0:
        vres = jnp.zeros((D_MODEL, 32), F32)
        vres_mu = jnp.zeros((32,), F32)
    else:
        vres = rw_vres_down[l - 1]
        vres_mu = rw_vres_mu[l - 1]
    misc = jnp.concatenate([gate_dn, vres, jnp.zeros((D_MODEL, LANE - 48), F32)], axis=1)
    w_in_p = jnp.concatenate([wi[:, :RW_COLS + HG_COLS], _pad_heads(q, 4, 48, 64), _pad_heads(k, 4, 48, 64),
                              _pad_heads(v, 4, 96, 128), _pad_heads(og, 4, 96, 128), misc], axis=1).astype(BF16)
    wo = w_out[l]
    wo_gl = jnp.pad(wo[640:].reshape(4, 96, D_MODEL), ((0, 0), (0, 32), (0, 0))).reshape(512, D_MODEL)
    w_out_p = jnp.concatenate([wo[:640], wo_gl], axis=0).astype(BF16)
    zeros64 = jnp.zeros((64, RW_W), F32)
    conv = gl_conv[l]
    conv_p = jnp.concatenate([_pad_heads(conv[:, 0:192], 4, 48, 64), _pad_heads(conv[:, 192:384], 4, 48, 64),
                              _pad_heads(conv[:, 384:768], 4, 96, 128)], axis=1)
    lw = {
        "w_in": w_in_p,
        "w_out": w_out_p,
        "rw_mu": rw_mu[l][None],
        "misc_mu": jnp.concatenate([jnp.zeros((16,), F32), vres_mu, jnp.zeros((LANE - 48,), F32)])[None],
        "w_up": jnp.concatenate([rw_w_up[l], zeros64], axis=0).astype(BF16),
        "a_up": jnp.concatenate([zeros64, rw_a_up[l]], axis=0).astype(BF16),
        "g_up": rw_g_up[l].astype(BF16),
        "rwp": jnp.stack([rw_w0[l], rw_a0[l], rw_k_k[l], rw_k_a[l], rw_r_k[l].reshape(RW_W), rw_ln_w[l], rw_ln_b[l],
                          rw_v0[l - 1] if l > 0 else jnp.zeros((RW_W,), F32)], axis=0),
        "lb_logits": hg_lb_logits,
        "hg_norm": hg_norm[l][None],
        "gl_conv": conv_p,
        "gate_up": jnp.pad(_pad_heads(gl_gate_up[l], 4, 48, 64), ((0, LANE - 16), (0, 0))).astype(BF16),
        "gate_b": _pad_heads(gl_gate_b[l], 4, 48, 64)[None],
        "gl_norm": _pad_heads(gl_norm[l], 4, 96, 128)[None],
    }
    if l > 0:
        lw["vres_up"] = jnp.pad(rw_vres_up[l - 1], ((16, LANE - 48), (0, 0))).astype(BF16)
    return lw


def kernel(x, ffn1_norm, ffn1_w_gate, ffn1_w_up, ffn1_w_down, mix_norm, w_in, w_out, rw_mu, rw_w0, rw_w_up, rw_a0, rw_a_up, rw_g_up, rw_k_k, rw_k_a, rw_r_k, rw_ln_w, rw_ln_b, rw_vres_down, rw_vres_mu, rw_v0, rw_vres_up, hg_lb_logits, hg_norm, gl_conv, gl_gate_up, gl_gate_b, gl_norm, ffn2_norm, ffn2_w_gate, ffn2_w_up, ffn2_w_down, final_norm):
    bsz, seq, d = x.shape
    assert bsz == 1 and d == D_MODEL and seq % TM_FFN == 0
    consts = _mixer_constants()
    h = x.reshape(seq, d)
    v_first = None
    for l in range(DEPTH):
        lw = _layer_weights(l, w_in, w_out, rw_mu, rw_w0, rw_w_up, rw_a0, rw_a_up, rw_g_up, rw_k_k, rw_k_a, rw_r_k,
                            rw_ln_w, rw_ln_b, rw_vres_down, rw_vres_mu, rw_v0, rw_vres_up, hg_lb_logits, hg_norm,
                            gl_conv, gl_gate_up, gl_gate_b, gl_norm)
        h = _ffn(h, ffn1_norm[l][None], ffn1_w_gate[l].astype(BF16), ffn1_w_up[l].astype(BF16),
                 ffn1_w_down[l].astype(BF16))
        proj = _proj(h, mix_norm[l][None], lw["w_in"])
        h, v_first = _mix(l, proj, h, v_first, lw, consts)
        h = _ffn(h, ffn2_norm[l][None], ffn2_w_gate[l].astype(BF16), ffn2_w_up[l].astype(BF16),
                 ffn2_w_down[l].astype(BF16), final_gain=final_norm[None] if l == DEPTH - 1 else None)
    return h.reshape(bsz, seq, d)
```

```python
import functools

import numpy as np
import jax
import jax.numpy as jnp
from jax import lax
from jax.experimental import pallas as pl
from jax.experimental.pallas import tpu as pltpu

F32 = jnp.float32
BF16 = jnp.bfloat16

D_MODEL = 1024
DEPTH = 4
D_FF = 2816
NORM_EPS = 1e-5
L2_EPS = 1e-12
RW_GN_EPS = 64e-5
HG_GATE_FLOOR = 1e-30
GL_GATE_NORMALIZER = 16.0
GL_KEY_DIM = 48
GL_VAL_DIM = 96
LOG2E = 1.4426950408889634

LANE = 128
SUBLANE = 8
VMEM_LIMIT = 56 * 1024 * 1024

RW_OFF, RW_COLS = 0, 1408
HG_OFF, HG_COLS = 1408, 1536
GL_OFF = 2944
GL_QKV = 1024
GL_OG_OFF = GL_OFF + GL_QKV
MISC_OFF = GL_OG_OFF + 512
NP = MISC_OFF + LANE
RW_W = 384

TT = 256
GC = 128
RC = 64
N_LEVELS = 7
TM_FFN = 512
TM_PROJ = 512


def _mm(a, b):
    return jnp.dot(a.astype(BF16), b.astype(BF16), preferred_element_type=F32)


def _mm_nt(a, b):
    return lax.dot_general(a.astype(BF16), b.astype(BF16), (((1,), (1,)), ((), ())),
                           preferred_element_type=F32)


def _mm_tn(a, b):
    return lax.dot_general(a.astype(BF16), b.astype(BF16), (((0,), (0,)), ((), ())),
                           preferred_element_type=F32)


def _sel_mm(c2, x):
    hi = x.astype(BF16)
    lo = (x - hi.astype(F32)).astype(BF16)
    return jnp.dot(c2, jnp.concatenate([hi, lo], axis=0), preferred_element_type=F32)


def _segsum(x, seg2):
    hi = x.astype(BF16)
    lo = (x - hi.astype(F32)).astype(BF16)
    cols = [jnp.dot(jnp.concatenate([hi[:, t:t + LANE], lo[:, t:t + LANE]], axis=1), seg2,
                    preferred_element_type=F32) for t in range(0, x.shape[1], LANE)]
    return jnp.concatenate(cols, axis=1)


def _sigmoid(x):
    return 0.5 * jnp.tanh(0.5 * x) + 0.5


def _silu(x):
    h = 0.5 * x
    return h * jnp.tanh(h) + h


def _softplus(x):
    return jnp.maximum(x, 0.0) + jnp.log(1.0 + jnp.exp(-jnp.abs(x)))


def _rms(x, gain):
    return x * lax.rsqrt(jnp.mean(x * x, axis=-1, keepdims=True) + NORM_EPS) * gain


def _push_rows(hist_ref, z):
    hist_ref[SUBLANE:SUBLANE + z.shape[0], :] = z


def _shifted(hist_ref, k, n):
    return hist_ref[SUBLANE - k:SUBLANE - k + n, :]


def _roll_history(hist_ref, n):
    hist_ref[0:SUBLANE, :] = hist_ref[n:n + SUBLANE, :]


def _interleave(*seqs):
    n = max(len(s) for s in seqs)
    for i in range(n):
        for s in seqs:
            for j in range(i * len(s) // n, (i + 1) * len(s) // n):
                s[j]()


def _const_spec(shape):
    nd = len(shape)
    return pl.BlockSpec(shape, lambda i, _nd=nd: (0,) * _nd, pipeline_mode=pl.Buffered(1))


def _layer_spec(arr, layer):
    nd = arr.ndim
    return pl.BlockSpec((None,) + arr.shape[1:], lambda i, _l=layer, _nd=nd: (_l,) + (0,) * (_nd - 1),
                        pipeline_mode=pl.Buffered(1))


def _ffn_body(*refs, final):
    if final:
        x_ref, g_ref, wg_ref, wu_ref, wd_ref, fg_ref, o_ref = refs
    else:
        x_ref, g_ref, wg_ref, wu_ref, wd_ref, o_ref = refs
    x = x_ref[...]
    h = _rms(x, g_ref[...]).astype(BF16)
    gate = jnp.dot(h, wg_ref[...], preferred_element_type=F32)
    up = jnp.dot(h, wu_ref[...], preferred_element_type=F32)
    act = (_silu(gate) * up).astype(BF16)
    y = x + 0.5 * jnp.dot(act, wd_ref[...], preferred_element_type=F32)
    if final:
        y = _rms(y, fg_ref[...])
    o_ref[...] = y


def _ffn(x, layer, gain, wg, wu, wd, final_gain=None):
    t = x.shape[0]
    final = final_gain is not None
    in_specs = [pl.BlockSpec((TM_FFN, D_MODEL), lambda i: (i, 0))] + [_layer_spec(a, layer) for a in (gain, wg, wu, wd)]
    args = [x, gain, wg, wu, wd]
    if final:
        in_specs.append(_const_spec((1, D_MODEL)))
        args.append(final_gain)
    return pl.pallas_call(
        functools.partial(_ffn_body, final=final),
        grid=(t // TM_FFN,),
        in_specs=in_specs,
        out_specs=pl.BlockSpec((TM_FFN, D_MODEL), lambda i: (i, 0)),
        out_shape=jax.ShapeDtypeStruct((t, D_MODEL), F32),
        compiler_params=pltpu.CompilerParams(dimension_semantics=("arbitrary",),
                                             vmem_limit_bytes=VMEM_LIMIT),
        name="ffn_final" if final else "ffn",
    )(*args)


def _proj_body(x_ref, g_ref, w_ref, o_ref):
    h = _rms(x_ref[...], g_ref[...]).astype(BF16)
    o_ref[...] = jnp.dot(h, w_ref[...], preferred_element_type=F32)


def _proj(x, layer, gain, w):
    t = x.shape[0]
    return pl.pallas_call(
        _proj_body,
        grid=(t // TM_PROJ,),
        in_specs=[pl.BlockSpec((TM_PROJ, D_MODEL), lambda i: (i, 0)), _layer_spec(gain, layer), _layer_spec(w, layer)],
        out_specs=pl.BlockSpec((TM_PROJ, NP), lambda i: (i, 0)),
        out_shape=jax.ShapeDtypeStruct((t, NP), F32),
        compiler_params=pltpu.CompilerParams(dimension_semantics=("arbitrary",),
                                             vmem_limit_bytes=VMEM_LIMIT),
        name="proj",
    )(x, gain, w)


def _decay_levels(log_d, g):
    kdim = log_d.shape[1]
    g = g * LOG2E
    log_d = log_d * LOG2E
    levels = []
    b = GC // 2
    while b >= 4:
        mids = [jnp.broadcast_to(g[m:m + 1], (2 * b, kdim)) for m in range(b, GC, 2 * b)]
        g_mid = mids[0] if len(mids) == 1 else jnp.concatenate(mids, axis=0)
        levels.append(jnp.exp2(-jnp.abs(g - g_mid)).astype(BF16))
        b //= 2
    up1 = pltpu.roll(log_d, GC - 1, 0)
    up2 = pltpu.roll(log_d, GC - 2, 0)
    r = lax.broadcasted_iota(jnp.int32, log_d.shape, 0)
    r4 = r & 3
    d2 = jnp.where(r4 == 0, up1 + up2, jnp.where(r4 == 1, up1, jnp.where(r4 == 2, 0.0, log_d)))
    levels.append(jnp.exp2(d2).astype(BF16))
    levels.append(jnp.exp2(jnp.where((r & 1) == 0, up1, 0.0)).astype(BF16))
    return levels, jnp.exp2(g), jnp.exp2(g[GC - 1:GC] - g)


def _mix_body(*refs, layer):
    first = layer == 0
    it = iter(refs)
    p_ref, x_ref = next(it), next(it)
    vf_ref = None if first else next(it)
    rwmu_ref, miscmu_ref, wup_ref, aup_ref, gup_ref = next(it), next(it), next(it), next(it), next(it)
    vup_ref = None if first else next(it)
    rwp_ref, lb_ref, hgn_ref = next(it), next(it), next(it)
    conv_ref, gateup_ref, gateb_ref, gln_ref, wout_ref = next(it), next(it), next(it), next(it), next(it)
    lev_ref, seg64_ref, seg128_ref, tri64_ref, tri128_ref = next(it), next(it), next(it), next(it), next(it)
    o_ref = next(it)
    vfo_ref = next(it) if first else None
    tail_rw, tail_misc, tail_conv, st_rw, st_hg, st_gl = (next(it) for _ in range(6))

    @pl.when(pl.program_id(0) == 0)
    def _():
        tail_rw[0:SUBLANE, :] = jnp.zeros((SUBLANE, RW_COLS), F32)
        tail_misc[0:SUBLANE, :] = jnp.zeros((SUBLANE, LANE), F32)
        tail_conv[0:SUBLANE, :] = jnp.zeros((SUBLANE, GL_QKV), F32)
        st_rw[...] = jnp.zeros_like(st_rw)
        st_hg[...] = jnp.zeros_like(st_hg)
        st_gl[...] = jnp.zeros_like(st_gl)

    m_lo = lax.broadcasted_iota(jnp.int32, (GC, LANE), 1) < 64
    row = lax.broadcasted_iota(jnp.int32, (GC, GC), 0)
    col = lax.broadcasted_iota(jnp.int32, (GC, GC), 1)
    strict = row > col
    incl = row >= col
    eye = jnp.where(row == col, 1.0, 0.0).astype(F32)
    lev = lev_ref[...]
    seg64 = seg64_ref[...]
    n_rc = TT // RC
    n_gc = TT // GC

    tri64 = tri64_ref[...]
    tri128 = tri128_ref[...]
    lane_rc = lax.broadcasted_iota(jnp.int32, (RC, LANE), 1)
    keep_lo = jnp.where(lane_rc < 64, 1.0, 0.0).astype(BF16)
    keep_hi = jnp.where(lane_rc < 64, 0.0, 1.0).astype(BF16)

    def cut(x, c, p):
        t = x[c * RC:(c + 1) * RC, p * LANE:(p + 1) * LANE]
        return jnp.concatenate([t * keep_lo, t * keep_hi], axis=0)

    probs = [(c, p) for c in range(n_rc) for p in range(3)]
    n_pr = len(probs)
    rw, hg, gl = {}, {}, {}

    def rw_gates():
        rw_z = p_ref[:, RW_OFF:RW_OFF + RW_COLS]
        _push_rows(tail_rw, rw_z)
        zs = rw_z + (_shifted(tail_rw, 1, TT) - rw_z) * rwmu_ref[...]
        _roll_history(tail_rw, TT)
        misc = p_ref[:, MISC_OFF:MISC_OFF + LANE]
        _push_rows(tail_misc, misc)
        misc = misc + (_shifted(tail_misc, 1, TT) - misc) * miscmu_ref[...]
        _roll_history(tail_misc, TT)
        rw["misc"] = misc

        r = zs[:, 0:384]
        k = zs[:, 384:768]
        v = zs[:, 768:1152]
        wa = zs[:, 1152:1280]
        g_dn = zs[:, 1280:1408]
        rwp = rwp_ref[...]
        w0, a0, k_k, k_a, r_k, ln_w, ln_b, v0 = (rwp[i:i + 1] for i in range(8))
        log_w = -jnp.exp(-_softplus(-(w0 + _mm(jnp.tanh(wa), wup_ref[...]))) - 0.5)
        a = _sigmoid(a0 + _mm(wa, aup_ref[...]))
        rw["g_out"] = _mm(_sigmoid(g_dn), gup_ref[...])
        if first:
            vfo_ref[...] = v
        else:
            v = v + (vf_ref[...] - v) * _sigmoid(v0 + _mm(misc, vup_ref[...]))
        kk = k * k_k
        kk = kk * jnp.minimum(lax.rsqrt(_segsum(kk * kk, seg64)), 1.0 / L2_EPS)
        k = k * (1.0 + (a - 1.0) * k_a)
        alpha = kk * a
        rw["bonus"] = _segsum(r * k * r_k, seg64) * v
        rw["ln"] = (ln_w, ln_b)

        g = jnp.concatenate([_sel_mm(tri64, log_w[c * GC:(c + 1) * GC]) for c in range(n_gc)], axis=0)
        gs = g * LOG2E
        gs_last = jnp.concatenate(
            [jnp.broadcast_to(gs[(c + 1) * RC - 1:(c + 1) * RC], (RC, RW_W)) for c in range(n_rc)], axis=0)
        kb, ab, kkb, vb = k.astype(BF16), alpha.astype(BF16), kk.astype(BF16), v.astype(BF16)
        inv = jnp.exp2(-gs).astype(BF16)
        dec = jnp.exp2(gs_last - gs).astype(BF16)
        k_in, a_in = kb * inv, ab * inv
        beta_d = kkb * jnp.exp2(gs - log_w * LOG2E).astype(BF16)
        r_d = r.astype(BF16) * jnp.exp2(gs).astype(BF16)
        k_end, a_end = kb * dec, ab * dec
        rw["chunk_decay"] = [jnp.exp2(gs[(c + 1) * RC - 1:(c + 1) * RC]) for c in range(n_rc)]
        rw["bd"] = [cut(beta_d, c, p) for c, p in probs]
        rw["rd"] = [cut(r_d, c, p) for c, p in probs]
        rw["vs"] = [cut(vb, c, p) for c, p in probs]
        rw["kin_ain"] = [jnp.concatenate([cut(k_in, c, p), cut(a_in, c, p)], axis=0) for c, p in probs]
        rw["aend"] = [cut(a_end, c, p) for c, p in probs]
        rw["kend"] = [cut(k_end, c, p) for c, p in probs]

    def rw_scores():
        sc = [_mm_nt(jnp.concatenate([rw["bd"][i], rw["rd"][i]], axis=0), rw["kin_ain"][i]) for i in range(n_pr)]
        rw["a_bk"] = [jnp.where(strict, s[:GC, :GC], 0.0).astype(BF16) for s in sc]
        rw["a_rk"] = [jnp.where(incl, s[GC:, :GC], 0.0).astype(BF16) for s in sc]
        rw["a_ra"] = [jnp.where(incl, s[GC:, GC:], 0.0).astype(BF16) for s in sc]
        rw["pw"] = [jnp.where(strict, -s[:GC, GC:], 0.0) for s in sc]
        rw["t_inv"] = [eye + x for x in rw["pw"]]

    def rw_square():
        rw["pw"] = [_mm(x, x) for x in rw["pw"]]
        rw["t_inv"] = [t + _mm(t, x) for t, x in zip(rw["t_inv"], rw["pw"])]

    def rw_abv():
        rw["abv"] = [_mm(rw["a_bk"][i], rw["vs"][i]) for i in range(n_pr)]

    def rw_solve():
        sol = [_mm(rw["t_inv"][i], jnp.concatenate([rw["bd"][i], rw["abv"][i].astype(BF16)], axis=1))
               for i in range(n_pr)]
        rw["w_s"] = [s[:, :LANE].astype(BF16) for s in sol]
        rw["u_s"] = [s[:, LANE:].astype(BF16) for s in sol]

    def rw_qeff():
        rw["q_eff"] = [rw["rd"][i].astype(F32) - _mm(rw["a_ra"][i], rw["w_s"][i]) for i in range(n_pr)]

    def rw_intra():
        rw["o"] = [_mm(jnp.concatenate([rw["a_rk"][i], -rw["a_ra"][i]], axis=1),
                       jnp.concatenate([rw["vs"][i], rw["u_s"][i]], axis=0)) for i in range(n_pr)]

    def rw_lowrank():
        rw["lowrank"] = [_mm_tn(rw["aend"][i], rw["w_s"][i]) for i in range(n_pr)]

    def rw_update():
        rw["upd_t"] = [_mm_tn(jnp.concatenate([rw["vs"][i], -rw["u_s"][i]], axis=0),
                              jnp.concatenate([rw["kend"][i], rw["aend"][i]], axis=0)) for i in range(n_pr)]
        rw["st"] = [st_rw[p] for p in range(3)]
        rw["y"] = {}

    def rw_step(c):
        def run():
            for p in range(3):
                i = c * 3 + p
                st = rw["st"][p]
                o_s = rw["o"][i] + _mm_nt(rw["q_eff"][i], st)
                rw["y"][(c, p)] = o_s[:RC] + o_s[RC:]
                rw["st"][p] = (st * rw["chunk_decay"][c][:, p * LANE:(p + 1) * LANE]
                               - _mm_nt(st, rw["lowrank"][i]) + rw["upd_t"][i])
        return run

    def rw_finish():
        for p in range(3):
            st_rw[p] = rw["st"][p]
        y = jnp.concatenate([jnp.concatenate([rw["y"][(c, p)] for p in range(3)], axis=1) for c in range(n_rc)],
                            axis=0)
        mean = _segsum(y, seg64) * (1.0 / 64)
        yc = y - mean
        var = _segsum(yc * yc, seg64) * (1.0 / 64)
        ln_w, ln_b = rw["ln"]
        y = yc * lax.rsqrt(var + RW_GN_EPS) * ln_w + ln_b
        rw["out"] = (y + rw["bonus"]) * rw["g_out"]

    rw_stages = ([rw_gates, rw_scores] + [rw_square] * 5 + [rw_abv, rw_solve, rw_qeff, rw_intra, rw_lowrank, rw_update]
                 + [rw_step(c) for c in range(n_rc)] + [rw_finish])

    def level_scores(q_t, k_t, levels, lanes):
        attn = jnp.zeros((GC, GC), F32)
        for lvl, e in enumerate(levels):
            e = e[:, lanes]
            attn = jnp.where(lev == lvl, _mm_nt(q_t * e, k_t * e), attn)
        return jnp.where(lev == N_LEVELS, _mm_nt(q_t, k_t), attn)

    def hg_gates():
        hq = _silu(p_ref[:, HG_OFF:HG_OFF + 512])
        hf = p_ref[:, HG_OFF + 512:HG_OFF + 1024]
        logits = lb_ref[...]
        ex = jnp.exp(logits - jnp.max(logits, axis=0, keepdims=True))
        prob = ex / jnp.sum(ex, axis=0, keepdims=True)
        lb = jnp.zeros((1, 512), F32)
        for i in range(1, layer + 1):
            lb = lb + prob[i:i + 1]
        ef = jnp.exp(-jnp.abs(hf))
        big = 1.0 / (1.0 + ef)
        small = ef * big
        pos = hf >= 0.0
        hg["log_f"] = jnp.log(jnp.maximum(lb + (1.0 - lb) * jnp.where(pos, big, small), HG_GATE_FLOOR))
        hk = (1.0 - lb) * jnp.where(pos, small, big)
        hg["q"], hg["k"], hg["qb"], hg["kb"] = hq, hk, hq.astype(BF16), hk.astype(BF16)
        hg["v"] = p_ref[:, HG_OFF + 1024:HG_OFF + 1280]
        hg["st"] = [st_hg[pr] for pr in range(2)]
        hg["rows"] = []

    def hg_levels(c):
        def run():
            lf = hg["log_f"][c * GC:(c + 1) * GC]
            hg[("e", c)] = _decay_levels(lf, _sel_mm(tri128, lf))
        return run

    def hg_scores(c, h):
        def run():
            rs = slice(c * GC, (c + 1) * GC)
            ls = slice(h * LANE, (h + 1) * LANE)
            hg[("attn", c, h)] = level_scores(hg["qb"][rs, ls], hg["kb"][rs, ls], hg[("e", c)][0], ls)
        return run

    bd_mask = (lax.broadcasted_iota(jnp.int32, (GC, 2 * LANE), 0) < 64) == (
        lax.broadcasted_iota(jnp.int32, (GC, 2 * LANE), 1) < LANE)

    def hg_chunk_out(c):
        def run():
            rs = slice(c * GC, (c + 1) * GC)
            _, e_incl, e_end = hg[("e", c)]
            cols = []
            for pr in range(2):
                ls2 = slice(pr * 2 * LANE, (pr + 1) * 2 * LANE)
                attn2 = jnp.concatenate([hg[("attn", c, 2 * pr)], hg[("attn", c, 2 * pr + 1)]], axis=1)
                vt = hg["v"][rs, pr * LANE:(pr + 1) * LANE]
                vt2 = jnp.concatenate([jnp.where(m_lo, vt, 0.0), jnp.where(m_lo, 0.0, vt)], axis=0)
                cols.append(_mm(attn2, vt2) + _mm_nt(hg["q"][rs, ls2] * e_incl[:, ls2], hg["st"][pr]))
            hg["rows"].append(jnp.concatenate(cols, axis=1))
            for pr in range(2):
                ls2 = slice(pr * 2 * LANE, (pr + 1) * 2 * LANE)
                vt = hg["v"][rs, pr * LANE:(pr + 1) * LANE]
                hg["st"][pr] = hg["st"][pr] * e_incl[GC - 1:GC, ls2] + jnp.where(
                    bd_mask, _mm_tn(vt, hg["k"][rs, ls2] * e_end[:, ls2]), 0.0)
        return run

    def hg_finish():
        for pr in range(2):
            st_hg[pr] = hg["st"][pr]
        ho = jnp.concatenate(hg["rows"], axis=0)
        ms = _segsum(ho * ho, seg64) * (1.0 / 64)
        hg["out"] = ho * lax.rsqrt(ms + NORM_EPS) * hgn_ref[...] * _silu(p_ref[:, HG_OFF + 1280:HG_OFF + 1536])

    hg_stages = [hg_gates]
    for c in range(n_gc):
        hg_stages += [hg_levels(c)] + [hg_scores(c, h) for h in range(4)] + [hg_chunk_out(c)]
    hg_stages.append(hg_finish)

    def gl_gates():
        cin = p_ref[:, GL_OFF:GL_OFF + GL_QKV]
        _push_rows(tail_conv, cin)
        cw = conv_ref[...]
        acc = cin * cw[3:4]
        for j in range(3):
            acc = acc + _shifted(tail_conv, 3 - j, TT) * cw[j:j + 1]
        _roll_history(tail_conv, TT)
        qkv = _silu(acc)
        gl["q"] = qkv[:, 0:256] * (GL_KEY_DIM ** -0.5)
        gl["k"] = qkv[:, 256:512]
        gl["v"] = qkv[:, 512:1024]
        gl["kb"] = gl["k"].astype(BF16)
        gl["log_a"] = -_softplus(-(_mm(rw["misc"], gateup_ref[...]) + gateb_ref[...])) * (1.0 / GL_GATE_NORMALIZER)
        gl["st"] = [st_gl[h] for h in range(4)]
        gl["rows"] = []

    def gl_levels(c):
        def run():
            la = gl["log_a"][c * GC:(c + 1) * GC]
            gl[("e", c)] = _decay_levels(la, _sel_mm(tri128, la))
        return run

    def gl_scores(c, h):
        def run():
            rs = slice(c * GC, (c + 1) * GC)
            ls = slice((h // 2) * LANE, (h // 2 + 1) * LANE)
            m = m_lo if h % 2 == 0 else jnp.logical_not(m_lo)
            gl[("attn", c, h)] = level_scores(jnp.where(m, gl["q"][rs, ls], 0.0).astype(BF16), gl["kb"][rs, ls],
                                              gl[("e", c)][0], ls)
        return run

    def gl_chunk_out(c):
        def run():
            rs = slice(c * GC, (c + 1) * GC)
            _, e_incl, e_end = gl[("e", c)]
            cols = []
            for h in range(4):
                ls = slice((h // 2) * LANE, (h // 2 + 1) * LANE)
                m = m_lo if h % 2 == 0 else jnp.logical_not(m_lo)
                cols.append(_mm(gl[("attn", c, h)], gl["v"][rs, h * LANE:(h + 1) * LANE])
                            + _mm_nt(jnp.where(m, gl["q"][rs, ls] * e_incl[:, ls], 0.0), gl["st"][h]))
            gl["rows"].append(jnp.concatenate(cols, axis=1))
            for h in range(4):
                ls = slice((h // 2) * LANE, (h // 2 + 1) * LANE)
                m = m_lo if h % 2 == 0 else jnp.logical_not(m_lo)
                gl["st"][h] = gl["st"][h] * e_incl[GC - 1:GC, ls] + _mm_tn(
                    gl["v"][rs, h * LANE:(h + 1) * LANE], jnp.where(m, gl["k"][rs, ls] * e_end[:, ls], 0.0))
        return run

    def gl_finish():
        for h in range(4):
            st_gl[h] = gl["st"][h]
        go = jnp.concatenate(gl["rows"], axis=0)
        ms = _segsum(go * go, seg128_ref[...]) * (1.0 / GL_VAL_DIM)
        gl["out"] = go * lax.rsqrt(ms + NORM_EPS) * gln_ref[...] * _silu(p_ref[:, GL_OG_OFF:GL_OG_OFF + 512])

    gl_stages = [gl_gates]
    for c in range(n_gc):
        gl_stages += [gl_levels(c)] + [gl_scores(c, h) for h in range(4)] + [gl_chunk_out(c)]
    gl_stages.append(gl_finish)

    _interleave(rw_stages, hg_stages, gl_stages)
    rw_out, hg_out, gl_out = rw["out"], hg["out"], gl["out"]

    mixed = jnp.concatenate([rw_out, hg_out, gl_out], axis=1).astype(BF16)
    o_ref[...] = x_ref[...] + jnp.dot(mixed, wout_ref[...], preferred_element_type=F32)


def _mix(layer, proj, x, v_first, lw, consts):
    t = x.shape[0]
    first = layer == 0
    tile = lambda n: pl.BlockSpec((TT, n), lambda i: (i, 0))
    args, in_specs = [proj, x], [tile(NP), tile(D_MODEL)]
    if not first:
        args.append(v_first)
        in_specs.append(tile(RW_W))
    names = ["rw_mu", "misc_mu", "w_up", "a_up", "g_up"] + ([] if first else ["vres_up"]) + [
        "rwp", "lb_logits", "hg_norm", "gl_conv", "gate_up", "gate_b", "gl_norm", "w_out"]
    for n in names:
        args.append(lw[n])
        in_specs.append(_const_spec(lw[n].shape) if n == "lb_logits" else _layer_spec(lw[n], layer))
    for n in ("lev", "seg64", "seg128", "tri64", "tri128"):
        args.append(consts[n])
        in_specs.append(_const_spec(consts[n].shape))
    out_shape = [jax.ShapeDtypeStruct((t, D_MODEL), F32)]
    out_specs = [tile(D_MODEL)]
    if first:
        out_shape.append(jax.ShapeDtypeStruct((t, RW_W), F32))
        out_specs.append(tile(RW_W))
    scratch = [
        pltpu.VMEM((TT + SUBLANE, RW_COLS), F32), pltpu.VMEM((TT + SUBLANE, LANE), F32),
        pltpu.VMEM((TT + SUBLANE, GL_QKV), F32),
        pltpu.VMEM((3, LANE, LANE), F32), pltpu.VMEM((2, LANE, 2 * LANE), F32), pltpu.VMEM((4, LANE, LANE), F32),
    ]
    res = pl.pallas_call(
        functools.partial(_mix_body, layer=layer),
        grid=(t // TT,),
        in_specs=in_specs,
        out_specs=out_specs,
        out_shape=out_shape,
        scratch_shapes=scratch,
        compiler_params=pltpu.CompilerParams(dimension_semantics=("arbitrary",),
                                             vmem_limit_bytes=VMEM_LIMIT),
        name="mixer0" if first else "mixer",
    )(*args)
    return (res[0], res[1]) if first else (res[0], v_first)


def _mixer_constants():
    s = np.arange(GC)[:, None]
    j = np.arange(GC)[None, :]
    x = s ^ j
    hb = np.where(x > 0, np.floor(np.log2(np.maximum(x, 1))).astype(np.int64), 0)
    lev = np.where(s > j, (N_LEVELS - 1) - hb, np.where(s == j, N_LEVELS, -1)).astype(np.int32)
    seg = lambda w: np.kron(np.eye(LANE // w, dtype=np.float32), np.ones((w, w), np.float32))
    tri64 = np.kron(np.eye(GC // RC, dtype=np.float32), np.tril(np.ones((RC, RC), np.float32)))
    tri128 = np.tril(np.ones((GC, GC), np.float32))
    return {
        "lev": jnp.asarray(lev),
        "seg64": jnp.asarray(np.concatenate([seg(64), seg(64)], axis=0), BF16),
        "seg128": jnp.asarray(np.concatenate([seg(128), seg(128)], axis=0), BF16),
        "tri64": jnp.asarray(np.concatenate([tri64, tri64], axis=1), BF16),
        "tri128": jnp.asarray(np.concatenate([tri128, tri128], axis=1), BF16),
    }


def _pad_heads(w, heads, dim, to):
    w = w.reshape(w.shape[:-1] + (heads, dim))
    w = jnp.pad(w, [(0, 0)] * (w.ndim - 1) + [(0, to - dim)])
    return w.reshape(w.shape[:-2] + (heads * to,))


def _mixer_weights(w_in, w_out, rw_mu, rw_w0, rw_w_up, rw_a0, rw_a_up, rw_g_up, rw_k_k, rw_k_a, rw_r_k,
                   rw_ln_w, rw_ln_b, rw_vres_down, rw_vres_mu, rw_v0, rw_vres_up, hg_lb_logits, hg_norm,
                   gl_conv, gl_gate_up, gl_gate_b, gl_norm):
    nl = DEPTH
    wi = w_in.astype(BF16)
    gl = wi[:, :, RW_COLS + HG_COLS:]
    q, k, v, gate_dn, og = (gl[:, :, 0:192], gl[:, :, 192:384], gl[:, :, 384:768], gl[:, :, 768:784],
                            gl[:, :, 784:1168])
    vres = jnp.concatenate([jnp.zeros((1, D_MODEL, 32), BF16), rw_vres_down.astype(BF16)], axis=0)
    vres_mu = jnp.concatenate([jnp.zeros((1, 32), F32), rw_vres_mu], axis=0)
    misc = jnp.concatenate([gate_dn, vres, jnp.zeros((nl, D_MODEL, LANE - 48), BF16)], axis=2)
    w_in_p = jnp.concatenate([wi[:, :, :RW_COLS + HG_COLS], _pad_heads(q, 4, 48, 64), _pad_heads(k, 4, 48, 64),
                              _pad_heads(v, 4, 96, 128), _pad_heads(og, 4, 96, 128), misc], axis=2)
    wo = w_out.astype(BF16)
    wo_gl = jnp.pad(wo[:, 640:].reshape(nl, 4, 96, D_MODEL), ((0, 0), (0, 0), (0, 32), (0, 0))).reshape(nl, 512, D_MODEL)
    zeros64 = jnp.zeros((nl, 64, RW_W), F32)
    conv_p = jnp.concatenate([_pad_heads(gl_conv[:, :, 0:192], 4, 48, 64), _pad_heads(gl_conv[:, :, 192:384], 4, 48, 64),
                              _pad_heads(gl_conv[:, :, 384:768], 4, 96, 128)], axis=2)
    v0 = jnp.concatenate([jnp.zeros((1, RW_W), F32), rw_v0], axis=0)
    vres_up = jnp.concatenate([jnp.zeros((1, 32, RW_W), F32), rw_vres_up], axis=0)
    return {
        "w_in": w_in_p,
        "w_out": jnp.concatenate([wo[:, :640], wo_gl], axis=1),
        "rw_mu": rw_mu[:, None],
        "misc_mu": jnp.concatenate([jnp.zeros((nl, 16), F32), vres_mu, jnp.zeros((nl, LANE - 48), F32)], axis=1)[:, None],
        "w_up": jnp.concatenate([rw_w_up, zeros64], axis=1).astype(BF16),
        "a_up": jnp.concatenate([zeros64, rw_a_up], axis=1).astype(BF16),
        "g_up": rw_g_up.astype(BF16),
        "vres_up": jnp.pad(vres_up, ((0, 0), (16, LANE - 48), (0, 0))).astype(BF16),
        "rwp": jnp.stack([rw_w0, rw_a0, rw_k_k, rw_k_a, rw_r_k.reshape(nl, RW_W), rw_ln_w, rw_ln_b, v0], axis=1),
        "lb_logits": hg_lb_logits,
        "hg_norm": hg_norm[:, None],
        "gl_conv": conv_p,
        "gate_up": jnp.pad(_pad_heads(gl_gate_up, 4, 48, 64), ((0, 0), (0, LANE - 16), (0, 0))).astype(BF16),
        "gate_b": _pad_heads(gl_gate_b, 4, 48, 64)[:, None],
        "gl_norm": _pad_heads(gl_norm, 4, 96, 128)[:, None],
    }


def kernel(x, ffn1_norm, ffn1_w_gate, ffn1_w_up, ffn1_w_down, mix_norm, w_in, w_out, rw_mu, rw_w0, rw_w_up, rw_a0, rw_a_up, rw_g_up, rw_k_k, rw_k_a, rw_r_k, rw_ln_w, rw_ln_b, rw_vres_down, rw_vres_mu, rw_v0, rw_vres_up, hg_lb_logits, hg_norm, gl_conv, gl_gate_up, gl_gate_b, gl_norm, ffn2_norm, ffn2_w_gate, ffn2_w_up, ffn2_w_down, final_norm):
    bsz, seq, d = x.shape
    assert bsz == 1 and d == D_MODEL and seq % TM_FFN == 0
    consts = _mixer_constants()
    lw = _mixer_weights(w_in, w_out, rw_mu, rw_w0, rw_w_up, rw_a0, rw_a_up, rw_g_up, rw_k_k, rw_k_a, rw_r_k,
                        rw_ln_w, rw_ln_b, rw_vres_down, rw_vres_mu, rw_v0, rw_vres_up, hg_lb_logits, hg_norm,
                        gl_conv, gl_gate_up, gl_gate_b, gl_norm)
    f1 = (ffn1_norm[:, None], ffn1_w_gate.astype(BF16), ffn1_w_up.astype(BF16), ffn1_w_down.astype(BF16))
    f2 = (ffn2_norm[:, None], ffn2_w_gate.astype(BF16), ffn2_w_up.astype(BF16), ffn2_w_down.astype(BF16))
    mix_gain = mix_norm[:, None]
    h = x.reshape(seq, d)
    v_first = None
    for l in range(DEPTH):
        h = _ffn(h, l, *f1)
        proj = _proj(h, l, mix_gain, lw["w_in"])
        h, v_first = _mix(l, proj, h, v_first, lw, consts)
        h = _ffn(h, l, *f2, final_gain=final_norm[None] if l == DEPTH - 1 else None)
    return h.reshape(bsz, seq, d)
```

```python
import functools

import numpy as np
import jax
import jax.numpy as jnp
from jax import lax
from jax.experimental import pallas as pl
from jax.experimental.pallas import tpu as pltpu

F32 = jnp.float32
BF16 = jnp.bfloat16

D_MODEL = 1024
DEPTH = 4
D_FF = 2816
NORM_EPS = 1e-5
L2_EPS = 1e-12
RW_GN_EPS = 64e-5
HG_GATE_FLOOR = 1e-30
GL_GATE_NORMALIZER = 16.0
GL_KEY_DIM = 48
GL_VAL_DIM = 96
LOG2E = 1.4426950408889634

LANE = 128
SUBLANE = 8
VMEM_LIMIT = 56 * 1024 * 1024

RW_OFF, RW_COLS = 0, 1408
HG_OFF, HG_COLS = 1408, 1536
GL_OFF = 2944
GL_QKV = 1024
GL_OG_OFF = GL_OFF + GL_QKV
MISC_OFF = GL_OG_OFF + 512
NP = MISC_OFF + LANE
RW_W = 384

TT = 512
GC = 128
RC = 64
N_LEVELS = 7
TM_FFN = 512
TM_PROJ = 512


def _mm(a, b):
    return jnp.dot(a.astype(BF16), b.astype(BF16), preferred_element_type=F32)


def _mm_nt(a, b):
    return lax.dot_general(a.astype(BF16), b.astype(BF16), (((1,), (1,)), ((), ())),
                           preferred_element_type=F32)


def _mm_tn(a, b):
    return lax.dot_general(a.astype(BF16), b.astype(BF16), (((0,), (0,)), ((), ())),
                           preferred_element_type=F32)


def _sel_mm(c2, x):
    hi = x.astype(BF16)
    lo = (x - hi.astype(F32)).astype(BF16)
    return jnp.dot(c2, jnp.concatenate([hi, lo], axis=0), preferred_element_type=F32)


def _segsum(x, seg2):
    hi = x.astype(BF16)
    lo = (x - hi.astype(F32)).astype(BF16)
    cols = [jnp.dot(jnp.concatenate([hi[:, t:t + LANE], lo[:, t:t + LANE]], axis=1), seg2,
                    preferred_element_type=F32) for t in range(0, x.shape[1], LANE)]
    return jnp.concatenate(cols, axis=1)


def _sigmoid(x):
    return 0.5 * jnp.tanh(0.5 * x) + 0.5


def _silu(x):
    h = 0.5 * x
    return h * jnp.tanh(h) + h


def _softplus(x):
    return jnp.maximum(x, 0.0) + jnp.log(1.0 + jnp.exp(-jnp.abs(x)))


def _rms(x, gain):
    return x * lax.rsqrt(jnp.mean(x * x, axis=-1, keepdims=True) + NORM_EPS) * gain


def _push_rows(hist_ref, z):
    hist_ref[SUBLANE:SUBLANE + z.shape[0], :] = z


def _shifted(hist_ref, k, n):
    return hist_ref[SUBLANE - k:SUBLANE - k + n, :]


def _roll_history(hist_ref, n):
    hist_ref[0:SUBLANE, :] = hist_ref[n:n + SUBLANE, :]


def _interleave(*seqs):
    n = max(len(s) for s in seqs)
    for i in range(n):
        for s in seqs:
            for j in range(-(-i * len(s) // n), -(-(i + 1) * len(s) // n)):
                s[j]()


def _const_spec(shape):
    nd = len(shape)
    return pl.BlockSpec(shape, lambda i, _nd=nd: (0,) * _nd, pipeline_mode=pl.Buffered(1))


def _layer_spec(arr, layer):
    nd = arr.ndim
    return pl.BlockSpec((None,) + arr.shape[1:], lambda i, _l=layer, _nd=nd: (_l,) + (0,) * (_nd - 1),
                        pipeline_mode=pl.Buffered(1))


def _ffn_body(*refs, final):
    if final:
        x_ref, g_ref, wg_ref, wu_ref, wd_ref, fg_ref, o_ref = refs
    else:
        x_ref, g_ref, wg_ref, wu_ref, wd_ref, o_ref = refs
    x = x_ref[...]
    h = _rms(x, g_ref[...]).astype(BF16)
    gate = jnp.dot(h, wg_ref[...], preferred_element_type=F32)
    up = jnp.dot(h, wu_ref[...], preferred_element_type=F32)
    act = (_silu(gate) * up).astype(BF16)
    y = x + 0.5 * jnp.dot(act, wd_ref[...], preferred_element_type=F32)
    if final:
        y = _rms(y, fg_ref[...])
    o_ref[...] = y


def _ffn(x, layer, gain, wg, wu, wd, final_gain=None):
    t = x.shape[0]
    final = final_gain is not None
    in_specs = [pl.BlockSpec((TM_FFN, D_MODEL), lambda i: (i, 0))] + [_layer_spec(a, layer) for a in (gain, wg, wu, wd)]
    args = [x, gain, wg, wu, wd]
    if final:
        in_specs.append(_const_spec((1, D_MODEL)))
        args.append(final_gain)
    return pl.pallas_call(
        functools.partial(_ffn_body, final=final),
        grid=(t // TM_FFN,),
        in_specs=in_specs,
        out_specs=pl.BlockSpec((TM_FFN, D_MODEL), lambda i: (i, 0)),
        out_shape=jax.ShapeDtypeStruct((t, D_MODEL), F32),
        compiler_params=pltpu.CompilerParams(dimension_semantics=("arbitrary",),
                                             vmem_limit_bytes=VMEM_LIMIT),
        name="ffn_final" if final else "ffn",
    )(*args)


def _proj_body(x_ref, g_ref, w_ref, o_ref):
    h = _rms(x_ref[...], g_ref[...]).astype(BF16)
    o_ref[...] = jnp.dot(h, w_ref[...], preferred_element_type=F32)


def _proj(x, layer, gain, w):
    t = x.shape[0]
    return pl.pallas_call(
        _proj_body,
        grid=(t // TM_PROJ,),
        in_specs=[pl.BlockSpec((TM_PROJ, D_MODEL), lambda i: (i, 0)), _layer_spec(gain, layer), _layer_spec(w, layer)],
        out_specs=pl.BlockSpec((TM_PROJ, NP), lambda i: (i, 0)),
        out_shape=jax.ShapeDtypeStruct((t, NP), F32),
        compiler_params=pltpu.CompilerParams(dimension_semantics=("arbitrary",),
                                             vmem_limit_bytes=VMEM_LIMIT),
        name="proj",
    )(x, gain, w)


def _decay_levels(log_d, g):
    kdim = log_d.shape[1]
    g = g * LOG2E
    log_d = log_d * LOG2E
    levels = []
    b = GC // 2
    while b >= 4:
        mids = [jnp.broadcast_to(g[m:m + 1], (2 * b, kdim)) for m in range(b, GC, 2 * b)]
        g_mid = mids[0] if len(mids) == 1 else jnp.concatenate(mids, axis=0)
        levels.append(jnp.exp2(-jnp.abs(g - g_mid)).astype(BF16))
        b //= 2
    up1 = pltpu.roll(log_d, GC - 1, 0)
    up2 = pltpu.roll(log_d, GC - 2, 0)
    r = lax.broadcasted_iota(jnp.int32, log_d.shape, 0)
    r4 = r & 3
    d2 = jnp.where(r4 == 0, up1 + up2, jnp.where(r4 == 1, up1, jnp.where(r4 == 2, 0.0, log_d)))
    levels.append(jnp.exp2(d2).astype(BF16))
    levels.append(jnp.exp2(jnp.where((r & 1) == 0, up1, 0.0)).astype(BF16))
    return levels, jnp.exp2(g), jnp.exp2(g[GC - 1:GC] - g)


def _mix_body(*refs, layer):
    first = layer == 0
    it = iter(refs)
    p_ref, x_ref = next(it), next(it)
    vf_ref = None if first else next(it)
    rwmu_ref, miscmu_ref, wup_ref, aup_ref, gup_ref = next(it), next(it), next(it), next(it), next(it)
    vup_ref = None if first else next(it)
    rwp_ref, lb_ref, hgn_ref = next(it), next(it), next(it)
    conv_ref, gateup_ref, gateb_ref, gln_ref, wout_ref = next(it), next(it), next(it), next(it), next(it)
    lev_ref, seg64_ref, seg128_ref, tri64_ref, tri128_ref = next(it), next(it), next(it), next(it), next(it)
    o_ref = next(it)
    vfo_ref = next(it) if first else None
    tail_rw, tail_misc, tail_conv, st_rw, st_hg, st_gl = (next(it) for _ in range(6))

    @pl.when(pl.program_id(0) == 0)
    def _():
        tail_rw[0:SUBLANE, :] = jnp.zeros((SUBLANE, RW_COLS), F32)
        tail_misc[0:SUBLANE, :] = jnp.zeros((SUBLANE, LANE), F32)
        tail_conv[0:SUBLANE, :] = jnp.zeros((SUBLANE, GL_QKV), F32)
        st_rw[...] = jnp.zeros_like(st_rw)
        st_hg[...] = jnp.zeros_like(st_hg)
        st_gl[...] = jnp.zeros_like(st_gl)

    m_lo = lax.broadcasted_iota(jnp.int32, (GC, LANE), 1) < 64
    row = lax.broadcasted_iota(jnp.int32, (GC, GC), 0)
    col = lax.broadcasted_iota(jnp.int32, (GC, GC), 1)
    strict = row > col
    incl = row >= col
    eye = jnp.where(row == col, 1.0, 0.0).astype(F32)
    lev = lev_ref[...]
    seg64 = seg64_ref[...]
    n_rc = TT // RC
    n_gc = TT // GC

    tri64 = tri64_ref[...]
    tri128 = tri128_ref[...]
    lane_rc = lax.broadcasted_iota(jnp.int32, (RC, LANE), 1)
    keep_lo = jnp.where(lane_rc < 64, 1.0, 0.0).astype(BF16)
    keep_hi = jnp.where(lane_rc < 64, 0.0, 1.0).astype(BF16)

    def cut(x, c, p):
        t = x[c * RC:(c + 1) * RC, p * LANE:(p + 1) * LANE]
        return jnp.concatenate([t * keep_lo, t * keep_hi], axis=0)

    probs = [(c, p) for c in range(n_rc) for p in range(3)]
    n_pr = len(probs)
    rw, hg, gl = {}, {}, {}

    def rw_gates():
        rw_z = p_ref[:, RW_OFF:RW_OFF + RW_COLS]
        _push_rows(tail_rw, rw_z)
        zs = rw_z + (_shifted(tail_rw, 1, TT) - rw_z) * rwmu_ref[...]
        _roll_history(tail_rw, TT)
        misc = p_ref[:, MISC_OFF:MISC_OFF + LANE]
        _push_rows(tail_misc, misc)
        misc = misc + (_shifted(tail_misc, 1, TT) - misc) * miscmu_ref[...]
        _roll_history(tail_misc, TT)
        rw["misc"] = misc

        r = zs[:, 0:384]
        k = zs[:, 384:768]
        v = zs[:, 768:1152]
        wa = zs[:, 1152:1280]
        g_dn = zs[:, 1280:1408]
        rwp = rwp_ref[...]
        w0, a0, k_k, k_a, r_k, ln_w, ln_b, v0 = (rwp[i:i + 1] for i in range(8))
        log_w = -jnp.exp(-_softplus(-(w0 + _mm(jnp.tanh(wa), wup_ref[...]))) - 0.5)
        a = _sigmoid(a0 + _mm(wa, aup_ref[...]))
        rw["g_out"] = _mm(_sigmoid(g_dn), gup_ref[...])
        if first:
            vfo_ref[...] = v
        else:
            v = v + (vf_ref[...] - v) * _sigmoid(v0 + _mm(misc, vup_ref[...]))
        kk = k * k_k
        kk = kk * jnp.minimum(lax.rsqrt(_segsum(kk * kk, seg64)), 1.0 / L2_EPS)
        k = k * (1.0 + (a - 1.0) * k_a)
        alpha = kk * a
        rw["bonus"] = _segsum(r * k * r_k, seg64) * v
        rw["ln"] = (ln_w, ln_b)

        g = jnp.concatenate([_sel_mm(tri64, log_w[c * GC:(c + 1) * GC]) for c in range(n_gc)], axis=0)
        gs = g * LOG2E
        gs_last = jnp.concatenate(
            [jnp.broadcast_to(gs[(c + 1) * RC - 1:(c + 1) * RC], (RC, RW_W)) for c in range(n_rc)], axis=0)
        kb, ab, kkb, vb = k.astype(BF16), alpha.astype(BF16), kk.astype(BF16), v.astype(BF16)
        inv = jnp.exp2(-gs).astype(BF16)
        dec = jnp.exp2(gs_last - gs).astype(BF16)
        k_in, a_in = kb * inv, ab * inv
        beta_d = kkb * jnp.exp2(gs - log_w * LOG2E).astype(BF16)
        r_d = r.astype(BF16) * jnp.exp2(gs).astype(BF16)
        k_end, a_end = kb * dec, ab * dec
        rw["chunk_decay"] = [jnp.exp2(gs[(c + 1) * RC - 1:(c + 1) * RC]) for c in range(n_rc)]
        rw["bd"] = [cut(beta_d, c, p) for c, p in probs]
        rw["rd"] = [cut(r_d, c, p) for c, p in probs]
        rw["vs"] = [cut(vb, c, p) for c, p in probs]
        rw["kin_ain"] = [jnp.concatenate([cut(k_in, c, p), cut(a_in, c, p)], axis=0) for c, p in probs]
        rw["aend"] = [cut(a_end, c, p) for c, p in probs]
        rw["kend"] = [cut(k_end, c, p) for c, p in probs]

    def rw_scores():
        sc = [_mm_nt(jnp.concatenate([rw["bd"][i], rw["rd"][i]], axis=0), rw["kin_ain"][i]) for i in range(n_pr)]
        rw["a_bk"] = [jnp.where(strict, s[:GC, :GC], 0.0).astype(BF16) for s in sc]
        rw["a_rk"] = [jnp.where(incl, s[GC:, :GC], 0.0).astype(BF16) for s in sc]
        rw["a_ra"] = [jnp.where(incl, s[GC:, GC:], 0.0).astype(BF16) for s in sc]
        rw["pw"] = [jnp.where(strict, -s[:GC, GC:], 0.0) for s in sc]
        rw["t_inv"] = [eye + x for x in rw["pw"]]

    def rw_square():
        rw["pw"] = [_mm(x, x) for x in rw["pw"]]
        rw["t_inv"] = [t + _mm(t, x) for t, x in zip(rw["t_inv"], rw["pw"])]

    def rw_abv():
        rw["abv"] = [_mm(rw["a_bk"][i], rw["vs"][i]) for i in range(n_pr)]

    def rw_solve():
        sol = [_mm(rw["t_inv"][i], jnp.concatenate([rw["bd"][i], rw["abv"][i].astype(BF16)], axis=1))
               for i in range(n_pr)]
        rw["w_s"] = [s[:, :LANE].astype(BF16) for s in sol]
        rw["u_s"] = [s[:, LANE:].astype(BF16) for s in sol]

    def rw_qeff():
        rw["q_eff"] = [rw["rd"][i].astype(F32) - _mm(rw["a_ra"][i], rw["w_s"][i]) for i in range(n_pr)]

    def rw_intra():
        rw["o"] = [_mm(jnp.concatenate([rw["a_rk"][i], -rw["a_ra"][i]], axis=1),
                       jnp.concatenate([rw["vs"][i], rw["u_s"][i]], axis=0)) for i in range(n_pr)]

    def rw_lowrank():
        rw["lowrank"] = [_mm_tn(rw["aend"][i], rw["w_s"][i]) for i in range(n_pr)]

    def rw_update():
        rw["upd_t"] = [_mm_tn(jnp.concatenate([rw["vs"][i], -rw["u_s"][i]], axis=0),
                              jnp.concatenate([rw["kend"][i], rw["aend"][i]], axis=0)) for i in range(n_pr)]
        rw["st"] = [st_rw[p] for p in range(3)]
        rw["y"] = {}

    def rw_step(c):
        def run():
            for p in range(3):
                i = c * 3 + p
                st = rw["st"][p]
                o_s = rw["o"][i] + _mm_nt(rw["q_eff"][i], st)
                rw["y"][(c, p)] = o_s[:RC] + o_s[RC:]
                rw["st"][p] = (st * rw["chunk_decay"][c][:, p * LANE:(p + 1) * LANE]
                               - _mm_nt(st, rw["lowrank"][i]) + rw["upd_t"][i])
        return run

    def rw_finish():
        for p in range(3):
            st_rw[p] = rw["st"][p]
        y = jnp.concatenate([jnp.concatenate([rw["y"][(c, p)] for p in range(3)], axis=1) for c in range(n_rc)],
                            axis=0)
        mean = _segsum(y, seg64) * (1.0 / 64)
        yc = y - mean
        var = _segsum(yc * yc, seg64) * (1.0 / 64)
        ln_w, ln_b = rw["ln"]
        y = yc * lax.rsqrt(var + RW_GN_EPS) * ln_w + ln_b
        rw["out"] = (y + rw["bonus"]) * rw["g_out"]

    rw_stages = ([rw_gates, rw_scores] + [rw_square] * 5 + [rw_abv, rw_solve, rw_qeff, rw_intra, rw_lowrank, rw_update]
                 + [rw_step(c) for c in range(n_rc)] + [rw_finish])

    def level_scores(q_t, k_t, levels, lanes):
        attn = jnp.zeros((GC, GC), F32)
        for lvl, e in enumerate(levels):
            e = e[:, lanes]
            attn = jnp.where(lev == lvl, _mm_nt(q_t * e, k_t * e), attn)
        return jnp.where(lev == N_LEVELS, _mm_nt(q_t, k_t), attn)

    def hg_gates():
        hq = _silu(p_ref[:, HG_OFF:HG_OFF + 512])
        hf = p_ref[:, HG_OFF + 512:HG_OFF + 1024]
        logits = lb_ref[...]
        ex = jnp.exp(logits - jnp.max(logits, axis=0, keepdims=True))
        prob = ex / jnp.sum(ex, axis=0, keepdims=True)
        lb = jnp.zeros((1, 512), F32)
        for i in range(1, layer + 1):
            lb = lb + prob[i:i + 1]
        ef = jnp.exp(-jnp.abs(hf))
        big = 1.0 / (1.0 + ef)
        small = ef * big
        pos = hf >= 0.0
        hg["log_f"] = jnp.log(jnp.maximum(lb + (1.0 - lb) * jnp.where(pos, big, small), HG_GATE_FLOOR))
        hk = (1.0 - lb) * jnp.where(pos, small, big)
        hg["q"], hg["k"], hg["qb"], hg["kb"] = hq, hk, hq.astype(BF16), hk.astype(BF16)
        hg["v"] = p_ref[:, HG_OFF + 1024:HG_OFF + 1280]
        hg["st"] = [st_hg[pr] for pr in range(2)]
        hg["rows"] = []

    def hg_levels(c):
        def run():
            lf = hg["log_f"][c * GC:(c + 1) * GC]
            hg[("e", c)] = _decay_levels(lf, _sel_mm(tri128, lf))
        return run

    def hg_scores(c, h):
        def run():
            rs = slice(c * GC, (c + 1) * GC)
            ls = slice(h * LANE, (h + 1) * LANE)
            hg[("attn", c, h)] = level_scores(hg["qb"][rs, ls], hg["kb"][rs, ls], hg[("e", c)][0], ls)
        return run

    bd_mask = (lax.broadcasted_iota(jnp.int32, (GC, 2 * LANE), 0) < 64) == (
        lax.broadcasted_iota(jnp.int32, (GC, 2 * LANE), 1) < LANE)

    def hg_chunk_out(c):
        def run():
            rs = slice(c * GC, (c + 1) * GC)
            _, e_incl, e_end = hg[("e", c)]
            cols = []
            for pr in range(2):
                ls2 = slice(pr * 2 * LANE, (pr + 1) * 2 * LANE)
                attn2 = jnp.concatenate([hg[("attn", c, 2 * pr)], hg[("attn", c, 2 * pr + 1)]], axis=1)
                vt = hg["v"][rs, pr * LANE:(pr + 1) * LANE]
                vt2 = jnp.concatenate([jnp.where(m_lo, vt, 0.0), jnp.where(m_lo, 0.0, vt)], axis=0)
                cols.append(_mm(attn2, vt2) + _mm_nt(hg["q"][rs, ls2] * e_incl[:, ls2], hg["st"][pr]))
            hg["rows"].append(jnp.concatenate(cols, axis=1))
            for pr in range(2):
                ls2 = slice(pr * 2 * LANE, (pr + 1) * 2 * LANE)
                vt = hg["v"][rs, pr * LANE:(pr + 1) * LANE]
                hg["st"][pr] = hg["st"][pr] * e_incl[GC - 1:GC, ls2] + jnp.where(
                    bd_mask, _mm_tn(vt, hg["k"][rs, ls2] * e_end[:, ls2]), 0.0)
        return run

    def hg_finish():
        for pr in range(2):
            st_hg[pr] = hg["st"][pr]
        ho = jnp.concatenate(hg["rows"], axis=0)
        ms = _segsum(ho * ho, seg64) * (1.0 / 64)
        hg["out"] = ho * lax.rsqrt(ms + NORM_EPS) * hgn_ref[...] * _silu(p_ref[:, HG_OFF + 1280:HG_OFF + 1536])

    hg_stages = [hg_gates]
    for c in range(n_gc):
        hg_stages += [hg_levels(c)] + [hg_scores(c, h) for h in range(4)] + [hg_chunk_out(c)]
    hg_stages.append(hg_finish)

    def gl_gates():
        cin = p_ref[:, GL_OFF:GL_OFF + GL_QKV]
        _push_rows(tail_conv, cin)
        cw = conv_ref[...]
        acc = cin * cw[3:4]
        for j in range(3):
            acc = acc + _shifted(tail_conv, 3 - j, TT) * cw[j:j + 1]
        _roll_history(tail_conv, TT)
        qkv = _silu(acc)
        gl["q"] = qkv[:, 0:256] * (GL_KEY_DIM ** -0.5)
        gl["k"] = qkv[:, 256:512]
        gl["v"] = qkv[:, 512:1024]
        gl["kb"] = gl["k"].astype(BF16)
        gl["log_a"] = -_softplus(-(_mm(rw["misc"], gateup_ref[...]) + gateb_ref[...])) * (1.0 / GL_GATE_NORMALIZER)
        gl["st"] = [st_gl[h] for h in range(4)]
        gl["rows"] = []

    def gl_levels(c):
        def run():
            la = gl["log_a"][c * GC:(c + 1) * GC]
            gl[("e", c)] = _decay_levels(la, _sel_mm(tri128, la))
        return run

    def gl_scores(c, h):
        def run():
            rs = slice(c * GC, (c + 1) * GC)
            ls = slice((h // 2) * LANE, (h // 2 + 1) * LANE)
            m = m_lo if h % 2 == 0 else jnp.logical_not(m_lo)
            gl[("attn", c, h)] = level_scores(jnp.where(m, gl["q"][rs, ls], 0.0).astype(BF16), gl["kb"][rs, ls],
                                              gl[("e", c)][0], ls)
        return run

    def gl_chunk_out(c):
        def run():
            rs = slice(c * GC, (c + 1) * GC)
            _, e_incl, e_end = gl[("e", c)]
            cols = []
            for h in range(4):
                ls = slice((h // 2) * LANE, (h // 2 + 1) * LANE)
                m = m_lo if h % 2 == 0 else jnp.logical_not(m_lo)
                cols.append(_mm(gl[("attn", c, h)], gl["v"][rs, h * LANE:(h + 1) * LANE])
                            + _mm_nt(jnp.where(m, gl["q"][rs, ls] * e_incl[:, ls], 0.0), gl["st"][h]))
            gl["rows"].append(jnp.concatenate(cols, axis=1))
            for h in range(4):
                ls = slice((h // 2) * LANE, (h // 2 + 1) * LANE)
                m = m_lo if h % 2 == 0 else jnp.logical_not(m_lo)
                gl["st"][h] = gl["st"][h] * e_incl[GC - 1:GC, ls] + _mm_tn(
                    gl["v"][rs, h * LANE:(h + 1) * LANE], jnp.where(m, gl["k"][rs, ls] * e_end[:, ls], 0.0))
        return run

    def gl_finish():
        for h in range(4):
            st_gl[h] = gl["st"][h]
        go = jnp.concatenate(gl["rows"], axis=0)
        ms = _segsum(go * go, seg128_ref[...]) * (1.0 / GL_VAL_DIM)
        gl["out"] = go * lax.rsqrt(ms + NORM_EPS) * gln_ref[...] * _silu(p_ref[:, GL_OG_OFF:GL_OG_OFF + 512])

    gl_stages = [gl_gates]
    for c in range(n_gc):
        gl_stages += [gl_levels(c)] + [gl_scores(c, h) for h in range(4)] + [gl_chunk_out(c)]
    gl_stages.append(gl_finish)

    _interleave(rw_stages, hg_stages, gl_stages)
    rw_out, hg_out, gl_out = rw["out"], hg["out"], gl["out"]

    mixed = jnp.concatenate([rw_out, hg_out, gl_out], axis=1).astype(BF16)
    o_ref[...] = x_ref[...] + jnp.dot(mixed, wout_ref[...], preferred_element_type=F32)


def _mix(layer, proj, x, v_first, lw, consts):
    t = x.shape[0]
    first = layer == 0
    tile = lambda n: pl.BlockSpec((TT, n), lambda i: (i, 0))
    args, in_specs = [proj, x], [tile(NP), tile(D_MODEL)]
    if not first:
        args.append(v_first)
        in_specs.append(tile(RW_W))
    names = ["rw_mu", "misc_mu", "w_up", "a_up", "g_up"] + ([] if first else ["vres_up"]) + [
        "rwp", "lb_logits", "hg_norm", "gl_conv", "gate_up", "gate_b", "gl_norm", "w_out"]
    for n in names:
        args.append(lw[n])
        in_specs.append(_const_spec(lw[n].shape) if n == "lb_logits" else _layer_spec(lw[n], layer))
    for n in ("lev", "seg64", "seg128", "tri64", "tri128"):
        args.append(consts[n])
        in_specs.append(_const_spec(consts[n].shape))
    out_shape = [jax.ShapeDtypeStruct((t, D_MODEL), F32)]
    out_specs = [tile(D_MODEL)]
    if first:
        out_shape.append(jax.ShapeDtypeStruct((t, RW_W), F32))
        out_specs.append(tile(RW_W))
    scratch = [
        pltpu.VMEM((TT + SUBLANE, RW_COLS), F32), pltpu.VMEM((TT + SUBLANE, LANE), F32),
        pltpu.VMEM((TT + SUBLANE, GL_QKV), F32),
        pltpu.VMEM((3, LANE, LANE), F32), pltpu.VMEM((2, LANE, 2 * LANE), F32), pltpu.VMEM((4, LANE, LANE), F32),
    ]
    res = pl.pallas_call(
        functools.partial(_mix_body, layer=layer),
        grid=(t // TT,),
        in_specs=in_specs,
        out_specs=out_specs,
        out_shape=out_shape,
        scratch_shapes=scratch,
        compiler_params=pltpu.CompilerParams(dimension_semantics=("arbitrary",),
                                             vmem_limit_bytes=VMEM_LIMIT),
        name="mixer0" if first else "mixer",
    )(*args)
    return (res[0], res[1]) if first else (res[0], v_first)


def _mixer_constants():
    s = np.arange(GC)[:, None]
    j = np.arange(GC)[None, :]
    x = s ^ j
    hb = np.where(x > 0, np.floor(np.log2(np.maximum(x, 1))).astype(np.int64), 0)
    lev = np.where(s > j, (N_LEVELS - 1) - hb, np.where(s == j, N_LEVELS, -1)).astype(np.int32)
    seg = lambda w: np.kron(np.eye(LANE // w, dtype=np.float32), np.ones((w, w), np.float32))
    tri64 = np.kron(np.eye(GC // RC, dtype=np.float32), np.tril(np.ones((RC, RC), np.float32)))
    tri128 = np.tril(np.ones((GC, GC), np.float32))
    return {
        "lev": jnp.asarray(lev),
        "seg64": jnp.asarray(np.concatenate([seg(64), seg(64)], axis=0), BF16),
        "seg128": jnp.asarray(np.concatenate([seg(128), seg(128)], axis=0), BF16),
        "tri64": jnp.asarray(np.concatenate([tri64, tri64], axis=1), BF16),
        "tri128": jnp.asarray(np.concatenate([tri128, tri128], axis=1), BF16),
    }


def _pad_heads(w, heads, dim, to):
    w = w.reshape(w.shape[:-1] + (heads, dim))
    w = jnp.pad(w, [(0, 0)] * (w.ndim - 1) + [(0, to - dim)])
    return w.reshape(w.shape[:-2] + (heads * to,))


def _mixer_weights(w_in, w_out, rw_mu, rw_w0, rw_w_up, rw_a0, rw_a_up, rw_g_up, rw_k_k, rw_k_a, rw_r_k,
                   rw_ln_w, rw_ln_b, rw_vres_down, rw_vres_mu, rw_v0, rw_vres_up, hg_lb_logits, hg_norm,
                   gl_conv, gl_gate_up, gl_gate_b, gl_norm):
    nl = DEPTH
    wi = w_in.astype(BF16)
    gl = wi[:, :, RW_COLS + HG_COLS:]
    q, k, v, gate_dn, og = (gl[:, :, 0:192], gl[:, :, 192:384], gl[:, :, 384:768], gl[:, :, 768:784],
                            gl[:, :, 784:1168])
    vres = jnp.concatenate([jnp.zeros((1, D_MODEL, 32), BF16), rw_vres_down.astype(BF16)], axis=0)
    vres_mu = jnp.concatenate([jnp.zeros((1, 32), F32), rw_vres_mu], axis=0)
    misc = jnp.concatenate([gate_dn, vres, jnp.zeros((nl, D_MODEL, LANE - 48), BF16)], axis=2)
    w_in_p = jnp.concatenate([wi[:, :, :RW_COLS + HG_COLS], _pad_heads(q, 4, 48, 64), _pad_heads(k, 4, 48, 64),
                              _pad_heads(v, 4, 96, 128), _pad_heads(og, 4, 96, 128), misc], axis=2)
    wo = w_out.astype(BF16)
    wo_gl = jnp.pad(wo[:, 640:].reshape(nl, 4, 96, D_MODEL), ((0, 0), (0, 0), (0, 32), (0, 0))).reshape(nl, 512, D_MODEL)
    zeros64 = jnp.zeros((nl, 64, RW_W), F32)
    conv_p = jnp.concatenate([_pad_heads(gl_conv[:, :, 0:192], 4, 48, 64), _pad_heads(gl_conv[:, :, 192:384], 4, 48, 64),
                              _pad_heads(gl_conv[:, :, 384:768], 4, 96, 128)], axis=2)
    v0 = jnp.concatenate([jnp.zeros((1, RW_W), F32), rw_v0], axis=0)
    vres_up = jnp.concatenate([jnp.zeros((1, 32, RW_W), F32), rw_vres_up], axis=0)
    return {
        "w_in": w_in_p,
        "w_out": jnp.concatenate([wo[:, :640], wo_gl], axis=1),
        "rw_mu": rw_mu[:, None],
        "misc_mu": jnp.concatenate([jnp.zeros((nl, 16), F32), vres_mu, jnp.zeros((nl, LANE - 48), F32)], axis=1)[:, None],
        "w_up": jnp.concatenate([rw_w_up, zeros64], axis=1).astype(BF16),
        "a_up": jnp.concatenate([zeros64, rw_a_up], axis=1).astype(BF16),
        "g_up": rw_g_up.astype(BF16),
        "vres_up": jnp.pad(vres_up, ((0, 0), (16, LANE - 48), (0, 0))).astype(BF16),
        "rwp": jnp.stack([rw_w0, rw_a0, rw_k_k, rw_k_a, rw_r_k.reshape(nl, RW_W), rw_ln_w, rw_ln_b, v0], axis=1),
        "lb_logits": hg_lb_logits,
        "hg_norm": hg_norm[:, None],
        "gl_conv": conv_p,
        "gate_up": jnp.pad(_pad_heads(gl_gate_up, 4, 48, 64), ((0, 0), (0, LANE - 16), (0, 0))).astype(BF16),
        "gate_b": _pad_heads(gl_gate_b, 4, 48, 64)[:, None],
        "gl_norm": _pad_heads(gl_norm, 4, 96, 128)[:, None],
    }


def kernel(x, ffn1_norm, ffn1_w_gate, ffn1_w_up, ffn1_w_down, mix_norm, w_in, w_out, rw_mu, rw_w0, rw_w_up, rw_a0, rw_a_up, rw_g_up, rw_k_k, rw_k_a, rw_r_k, rw_ln_w, rw_ln_b, rw_vres_down, rw_vres_mu, rw_v0, rw_vres_up, hg_lb_logits, hg_norm, gl_conv, gl_gate_up, gl_gate_b, gl_norm, ffn2_norm, ffn2_w_gate, ffn2_w_up, ffn2_w_down, final_norm):
    bsz, seq, d = x.shape
    assert bsz == 1 and d == D_MODEL and seq % TM_FFN == 0
    consts = _mixer_constants()
    lw = _mixer_weights(w_in, w_out, rw_mu, rw_w0, rw_w_up, rw_a0, rw_a_up, rw_g_up, rw_k_k, rw_k_a, rw_r_k,
                        rw_ln_w, rw_ln_b, rw_vres_down, rw_vres_mu, rw_v0, rw_vres_up, hg_lb_logits, hg_norm,
                        gl_conv, gl_gate_up, gl_gate_b, gl_norm)
    f1 = (ffn1_norm[:, None], ffn1_w_gate.astype(BF16), ffn1_w_up.astype(BF16), ffn1_w_down.astype(BF16))
    f2 = (ffn2_norm[:, None], ffn2_w_gate.astype(BF16), ffn2_w_up.astype(BF16), ffn2_w_down.astype(BF16))
    mix_gain = mix_norm[:, None]
    h = x.reshape(seq, d)
    v_first = None
    for l in range(DEPTH):
        h = _ffn(h, l, *f1)
        proj = _proj(h, l, mix_gain, lw["w_in"])
        h, v_first = _mix(l, proj, h, v_first, lw, consts)
        h = _ffn(h, l, *f2, final_gain=final_norm[None] if l == DEPTH - 1 else None)
    return h.reshape(bsz, seq, d)
```

```python
import functools

import numpy as np
import jax
import jax.numpy as jnp
from jax import lax
from jax.experimental import pallas as pl
from jax.experimental.pallas import tpu as pltpu

F32 = jnp.float32
BF16 = jnp.bfloat16

D_MODEL = 1024
DEPTH = 4
D_FF = 2816
NORM_EPS = 1e-5
L2_EPS = 1e-12
RW_GN_EPS = 64e-5
HG_GATE_FLOOR = 1e-30
GL_GATE_NORMALIZER = 16.0
GL_KEY_DIM = 48
GL_VAL_DIM = 96
LOG2E = 1.4426950408889634

LANE = 128
SUBLANE = 8
VMEM_LIMIT = 56 * 1024 * 1024

RW_OFF, RW_COLS = 0, 1408
HG_OFF, HG_COLS = 1408, 1536
GL_OFF = 2944
GL_QKV = 1024
GL_OG_OFF = GL_OFF + GL_QKV
MISC_OFF = GL_OG_OFF + 512
NP = MISC_OFF + LANE
PROJ_SPLIT = 2816
RW_W = 384

TT = 512
GC = 128
RC = 64
N_LEVELS = 7
TM_FFN = 512
TM_PROJ = 512


def _mm(a, b):
    return jnp.dot(a.astype(BF16), b.astype(BF16), preferred_element_type=F32)


def _mm_nt(a, b):
    return lax.dot_general(a.astype(BF16), b.astype(BF16), (((1,), (1,)), ((), ())),
                           preferred_element_type=F32)


def _mm_tn(a, b):
    return lax.dot_general(a.astype(BF16), b.astype(BF16), (((0,), (0,)), ((), ())),
                           preferred_element_type=F32)


def _sel_mm(c2, x):
    hi = x.astype(BF16)
    lo = (x - hi.astype(F32)).astype(BF16)
    return jnp.dot(c2, jnp.concatenate([hi, lo], axis=0), preferred_element_type=F32)


def _segsum(x, seg2):
    hi = x.astype(BF16)
    lo = (x - hi.astype(F32)).astype(BF16)
    cols = [jnp.dot(jnp.concatenate([hi[:, t:t + LANE], lo[:, t:t + LANE]], axis=1), seg2,
                    preferred_element_type=F32) for t in range(0, x.shape[1], LANE)]
    return jnp.concatenate(cols, axis=1)


def _sigmoid(x):
    return 0.5 * jnp.tanh(0.5 * x) + 0.5


def _silu(x):
    h = 0.5 * x
    return h * jnp.tanh(h) + h


def _softplus(x):
    return jnp.maximum(x, 0.0) + jnp.log(1.0 + jnp.exp(-jnp.abs(x)))


def _rms(x, gain):
    return x * lax.rsqrt(jnp.mean(x * x, axis=-1, keepdims=True) + NORM_EPS) * gain


def _push_rows(hist_ref, z):
    hist_ref[SUBLANE:SUBLANE + z.shape[0], :] = z


def _shifted(hist_ref, k, n):
    return hist_ref[SUBLANE - k:SUBLANE - k + n, :]


def _roll_history(hist_ref, n):
    hist_ref[0:SUBLANE, :] = hist_ref[n:n + SUBLANE, :]


def _interleave(*seqs):
    n = max(len(s) for s in seqs)
    for i in range(n):
        for s in seqs:
            for j in range(-(-i * len(s) // n), -(-(i + 1) * len(s) // n)):
                s[j]()


def _const_spec(shape):
    nd = len(shape)
    return pl.BlockSpec(shape, lambda i, _nd=nd: (0,) * _nd, pipeline_mode=pl.Buffered(1))


def _layer_spec(arr, layer):
    nd = arr.ndim
    return pl.BlockSpec((None,) + arr.shape[1:], lambda i, _l=layer, _nd=nd: (_l,) + (0,) * (_nd - 1),
                        pipeline_mode=pl.Buffered(1))


def _ffn_body(*refs, final):
    if final:
        x_ref, g_ref, wg_ref, wu_ref, wd_ref, fg_ref, o_ref = refs
    else:
        x_ref, g_ref, wg_ref, wu_ref, wd_ref, o_ref = refs
    x = x_ref[...]
    h = _rms(x, g_ref[...]).astype(BF16)
    gate = jnp.dot(h, wg_ref[...], preferred_element_type=F32)
    up = jnp.dot(h, wu_ref[...], preferred_element_type=F32)
    act = (_silu(gate) * up).astype(BF16)
    y = x + 0.5 * jnp.dot(act, wd_ref[...], preferred_element_type=F32)
    if final:
        y = _rms(y, fg_ref[...])
    o_ref[...] = y


def _ffn(x, layer, gain, wg, wu, wd, final_gain=None):
    t = x.shape[0]
    final = final_gain is not None
    in_specs = [pl.BlockSpec((TM_FFN, D_MODEL), lambda i: (i, 0))] + [_layer_spec(a, layer) for a in (gain, wg, wu, wd)]
    args = [x, gain, wg, wu, wd]
    if final:
        in_specs.append(_const_spec((1, D_MODEL)))
        args.append(final_gain)
    return pl.pallas_call(
        functools.partial(_ffn_body, final=final),
        grid=(t // TM_FFN,),
        in_specs=in_specs,
        out_specs=pl.BlockSpec((TM_FFN, D_MODEL), lambda i: (i, 0)),
        out_shape=jax.ShapeDtypeStruct((t, D_MODEL), F32),
        compiler_params=pltpu.CompilerParams(dimension_semantics=("arbitrary",),
                                             vmem_limit_bytes=VMEM_LIMIT),
        name="ffn_final" if final else "ffn",
    )(*args)


def _proj_body(x_ref, g_ref, wm_ref, wg_ref, o_ref):
    h = _rms(x_ref[...], g_ref[...]).astype(BF16)
    o_ref[:, :PROJ_SPLIT] = jnp.dot(h, wm_ref[...], preferred_element_type=F32)
    o_ref[:, PROJ_SPLIT:] = jnp.dot(h, wg_ref[...], preferred_element_type=F32)


def _proj(x, layer, gain, w_all, w_tail):
    t = x.shape[0]
    main_spec = pl.BlockSpec((None, D_MODEL, PROJ_SPLIT), lambda i: (layer, 0, 0), pipeline_mode=pl.Buffered(1))
    return pl.pallas_call(
        _proj_body,
        grid=(t // TM_PROJ,),
        in_specs=[pl.BlockSpec((TM_PROJ, D_MODEL), lambda i: (i, 0)), _layer_spec(gain, layer), main_spec,
                  _layer_spec(w_tail, layer)],
        out_specs=pl.BlockSpec((TM_PROJ, NP), lambda i: (i, 0)),
        out_shape=jax.ShapeDtypeStruct((t, NP), F32),
        compiler_params=pltpu.CompilerParams(dimension_semantics=("arbitrary",),
                                             vmem_limit_bytes=VMEM_LIMIT),
        name="proj",
    )(x, gain, w_all, w_tail)


def _decay_levels(log_d, g):
    kdim = log_d.shape[1]
    g = g * LOG2E
    log_d = log_d * LOG2E
    levels = []
    b = GC // 2
    while b >= 4:
        mids = [jnp.broadcast_to(g[m:m + 1], (2 * b, kdim)) for m in range(b, GC, 2 * b)]
        g_mid = mids[0] if len(mids) == 1 else jnp.concatenate(mids, axis=0)
        levels.append(jnp.exp2(-jnp.abs(g - g_mid)).astype(BF16))
        b //= 2
    up1 = pltpu.roll(log_d, GC - 1, 0)
    up2 = pltpu.roll(log_d, GC - 2, 0)
    r = lax.broadcasted_iota(jnp.int32, log_d.shape, 0)
    r4 = r & 3
    d2 = jnp.where(r4 == 0, up1 + up2, jnp.where(r4 == 1, up1, jnp.where(r4 == 2, 0.0, log_d)))
    levels.append(jnp.exp2(d2).astype(BF16))
    levels.append(jnp.exp2(jnp.where((r & 1) == 0, up1, 0.0)).astype(BF16))
    return levels, jnp.exp2(g), jnp.exp2(g[GC - 1:GC] - g)


def _mix_body(*refs, layer):
    first = layer == 0
    it = iter(refs)
    p_ref, x_ref = next(it), next(it)
    vf_ref = None if first else next(it)
    rwmu_ref, miscmu_ref, wup_ref, aup_ref, gup_ref = next(it), next(it), next(it), next(it), next(it)
    vup_ref = None if first else next(it)
    rwp_ref, lb_ref, hgn_ref = next(it), next(it), next(it)
    conv_ref, gateup_ref, gateb_ref, gln_ref, wout_ref = next(it), next(it), next(it), next(it), next(it)
    lev_ref, seg64_ref, seg128_ref, tri64_ref, tri128_ref = next(it), next(it), next(it), next(it), next(it)
    o_ref = next(it)
    vfo_ref = next(it) if first else None
    tail_rw, tail_misc, tail_conv, st_rw, st_hg, st_gl = (next(it) for _ in range(6))

    @pl.when(pl.program_id(0) == 0)
    def _():
        tail_rw[0:SUBLANE, :] = jnp.zeros((SUBLANE, RW_COLS), F32)
        tail_misc[0:SUBLANE, :] = jnp.zeros((SUBLANE, LANE), F32)
        tail_conv[0:SUBLANE, :] = jnp.zeros((SUBLANE, GL_QKV), F32)
        st_rw[...] = jnp.zeros_like(st_rw)
        st_hg[...] = jnp.zeros_like(st_hg)
        st_gl[...] = jnp.zeros_like(st_gl)

    m_lo = lax.broadcasted_iota(jnp.int32, (GC, LANE), 1) < 64
    row = lax.broadcasted_iota(jnp.int32, (GC, GC), 0)
    col = lax.broadcasted_iota(jnp.int32, (GC, GC), 1)
    strict = row > col
    incl = row >= col
    eye = jnp.where(row == col, 1.0, 0.0).astype(F32)
    lev = lev_ref[...]
    seg64 = seg64_ref[...]
    n_rc = TT // RC
    n_gc = TT // GC

    tri64 = tri64_ref[...]
    tri128 = tri128_ref[...]
    lane_rc = lax.broadcasted_iota(jnp.int32, (RC, LANE), 1)
    keep_lo = jnp.where(lane_rc < 64, 1.0, 0.0).astype(BF16)
    keep_hi = jnp.where(lane_rc < 64, 0.0, 1.0).astype(BF16)

    def cut(x, c, p):
        t = x[c * RC:(c + 1) * RC, p * LANE:(p + 1) * LANE]
        return jnp.concatenate([t * keep_lo, t * keep_hi], axis=0)

    probs = [(c, p) for c in range(n_rc) for p in range(3)]
    n_pr = len(probs)
    rw, hg, gl = {}, {}, {}

    def rw_gates():
        rw_z = p_ref[:, RW_OFF:RW_OFF + RW_COLS]
        _push_rows(tail_rw, rw_z)
        zs = rw_z + (_shifted(tail_rw, 1, TT) - rw_z) * rwmu_ref[...]
        _roll_history(tail_rw, TT)
        misc = p_ref[:, MISC_OFF:MISC_OFF + LANE]
        _push_rows(tail_misc, misc)
        misc = misc + (_shifted(tail_misc, 1, TT) - misc) * miscmu_ref[...]
        _roll_history(tail_misc, TT)
        rw["misc"] = misc

        r = zs[:, 0:384]
        k = zs[:, 384:768]
        v = zs[:, 768:1152]
        wa = zs[:, 1152:1280]
        g_dn = zs[:, 1280:1408]
        rwp = rwp_ref[...]
        w0, a0, k_k, k_a, r_k, ln_w, ln_b, v0 = (rwp[i:i + 1] for i in range(8))
        log_w = -jnp.exp(-_softplus(-(w0 + _mm(jnp.tanh(wa), wup_ref[...]))) - 0.5)
        a = _sigmoid(a0 + _mm(wa, aup_ref[...]))
        rw["g_out"] = _mm(_sigmoid(g_dn), gup_ref[...])
        if first:
            vfo_ref[...] = v
        else:
            v = v + (vf_ref[...] - v) * _sigmoid(v0 + _mm(misc, vup_ref[...]))
        kk = k * k_k
        kk = kk * jnp.minimum(lax.rsqrt(_segsum(kk * kk, seg64)), 1.0 / L2_EPS)
        k = k * (1.0 + (a - 1.0) * k_a)
        alpha = kk * a
        rw["bonus"] = _segsum(r * k * r_k, seg64) * v
        rw["ln"] = (ln_w, ln_b)

        g = jnp.concatenate([_sel_mm(tri64, log_w[c * GC:(c + 1) * GC]) for c in range(n_gc)], axis=0)
        gs = g * LOG2E
        gs_last = jnp.concatenate(
            [jnp.broadcast_to(gs[(c + 1) * RC - 1:(c + 1) * RC], (RC, RW_W)) for c in range(n_rc)], axis=0)
        kb, ab, kkb, vb = k.astype(BF16), alpha.astype(BF16), kk.astype(BF16), v.astype(BF16)
        inv = jnp.exp2(-gs).astype(BF16)
        dec = jnp.exp2(gs_last - gs).astype(BF16)
        k_in, a_in = kb * inv, ab * inv
        beta_d = kkb * jnp.exp2(gs - log_w * LOG2E).astype(BF16)
        r_d = r.astype(BF16) * jnp.exp2(gs).astype(BF16)
        k_end, a_end = kb * dec, ab * dec
        rw["chunk_decay"] = [jnp.exp2(gs[(c + 1) * RC - 1:(c + 1) * RC]) for c in range(n_rc)]
        rw["bd"] = [cut(beta_d, c, p) for c, p in probs]
        rw["rd"] = [cut(r_d, c, p) for c, p in probs]
        rw["vs"] = [cut(vb, c, p) for c, p in probs]
        rw["kin_ain"] = [jnp.concatenate([cut(k_in, c, p), cut(a_in, c, p)], axis=0) for c, p in probs]
        rw["aend"] = [cut(a_end, c, p) for c, p in probs]
        rw["kend"] = [cut(k_end, c, p) for c, p in probs]

    def rw_scores():
        sc = [_mm_nt(jnp.concatenate([rw["bd"][i], rw["rd"][i]], axis=0), rw["kin_ain"][i]) for i in range(n_pr)]
        rw["a_bk"] = [jnp.where(strict, s[:GC, :GC], 0.0).astype(BF16) for s in sc]
        rw["a_rk"] = [jnp.where(incl, s[GC:, :GC], 0.0).astype(BF16) for s in sc]
        rw["a_ra"] = [jnp.where(incl, s[GC:, GC:], 0.0).astype(BF16) for s in sc]
        rw["pw"] = [jnp.where(strict, -s[:GC, GC:], 0.0) for s in sc]
        rw["t_inv"] = [eye + x for x in rw["pw"]]

    def rw_square():
        rw["pw"] = [_mm(x, x) for x in rw["pw"]]
        rw["t_inv"] = [t + _mm(t, x) for t, x in zip(rw["t_inv"], rw["pw"])]

    def rw_abv():
        rw["abv"] = [_mm(rw["a_bk"][i], rw["vs"][i]) for i in range(n_pr)]

    def rw_solve():
        sol = [_mm(rw["t_inv"][i], jnp.concatenate([rw["bd"][i], rw["abv"][i].astype(BF16)], axis=1))
               for i in range(n_pr)]
        rw["w_s"] = [s[:, :LANE].astype(BF16) for s in sol]
        rw["u_s"] = [s[:, LANE:].astype(BF16) for s in sol]

    def rw_qeff():
        rw["q_eff"] = [rw["rd"][i].astype(F32) - _mm(rw["a_ra"][i], rw["w_s"][i]) for i in range(n_pr)]

    def rw_intra():
        rw["o"] = [_mm(jnp.concatenate([rw["a_rk"][i], -rw["a_ra"][i]], axis=1),
                       jnp.concatenate([rw["vs"][i], rw["u_s"][i]], axis=0)) for i in range(n_pr)]

    def rw_lowrank():
        rw["lowrank"] = [_mm_tn(rw["aend"][i], rw["w_s"][i]) for i in range(n_pr)]

    def rw_update():
        rw["upd_t"] = [_mm_tn(jnp.concatenate([rw["vs"][i], -rw["u_s"][i]], axis=0),
                              jnp.concatenate([rw["kend"][i], rw["aend"][i]], axis=0)) for i in range(n_pr)]
        rw["st"] = [st_rw[p] for p in range(3)]
        rw["y"] = {}

    def rw_step(c):
        def run():
            for p in range(3):
                i = c * 3 + p
                st = rw["st"][p]
                o_s = rw["o"][i] + _mm_nt(rw["q_eff"][i], st)
                rw["y"][(c, p)] = o_s[:RC] + o_s[RC:]
                rw["st"][p] = (st * rw["chunk_decay"][c][:, p * LANE:(p + 1) * LANE]
                               - _mm_nt(st, rw["lowrank"][i]) + rw["upd_t"][i])
        return run

    def rw_finish():
        for p in range(3):
            st_rw[p] = rw["st"][p]
        y = jnp.concatenate([jnp.concatenate([rw["y"][(c, p)] for p in range(3)], axis=1) for c in range(n_rc)],
                            axis=0)
        mean = _segsum(y, seg64) * (1.0 / 64)
        yc = y - mean
        var = _segsum(yc * yc, seg64) * (1.0 / 64)
        ln_w, ln_b = rw["ln"]
        y = yc * lax.rsqrt(var + RW_GN_EPS) * ln_w + ln_b
        rw["out"] = (y + rw["bonus"]) * rw["g_out"]

    rw_stages = ([rw_gates, rw_scores] + [rw_square] * 5 + [rw_abv, rw_solve, rw_qeff, rw_intra, rw_lowrank, rw_update]
                 + [rw_step(c) for c in range(n_rc)] + [rw_finish])

    def level_scores(q_t, k_t, levels, lanes):
        attn = jnp.zeros((GC, GC), F32)
        for lvl, e in enumerate(levels):
            e = e[:, lanes]
            attn = jnp.where(lev == lvl, _mm_nt(q_t * e, k_t * e), attn)
        return jnp.where(lev == N_LEVELS, _mm_nt(q_t, k_t), attn)

    def hg_gates():
        hq = _silu(p_ref[:, HG_OFF:HG_OFF + 512])
        hf = p_ref[:, HG_OFF + 512:HG_OFF + 1024]
        logits = lb_ref[...]
        ex = jnp.exp(logits - jnp.max(logits, axis=0, keepdims=True))
        prob = ex / jnp.sum(ex, axis=0, keepdims=True)
        lb = jnp.zeros((1, 512), F32)
        for i in range(1, layer + 1):
            lb = lb + prob[i:i + 1]
        ef = jnp.exp(-jnp.abs(hf))
        big = 1.0 / (1.0 + ef)
        small = ef * big
        pos = hf >= 0.0
        hg["log_f"] = jnp.log(jnp.maximum(lb + (1.0 - lb) * jnp.where(pos, big, small), HG_GATE_FLOOR))
        hk = (1.0 - lb) * jnp.where(pos, small, big)
        hg["q"], hg["k"], hg["qb"], hg["kb"] = hq, hk, hq.astype(BF16), hk.astype(BF16)
        hg["v"] = p_ref[:, HG_OFF + 1024:HG_OFF + 1280]
        hg["st"] = [st_hg[pr] for pr in range(2)]
        hg["rows"] = []

    def hg_levels(c):
        def run():
            lf = hg["log_f"][c * GC:(c + 1) * GC]
            hg[("e", c)] = _decay_levels(lf, _sel_mm(tri128, lf))
        return run

    def hg_scores(c, h):
        def run():
            rs = slice(c * GC, (c + 1) * GC)
            ls = slice(h * LANE, (h + 1) * LANE)
            hg[("attn", c, h)] = level_scores(hg["qb"][rs, ls], hg["kb"][rs, ls], hg[("e", c)][0], ls)
        return run

    bd_mask = (lax.broadcasted_iota(jnp.int32, (GC, 2 * LANE), 0) < 64) == (
        lax.broadcasted_iota(jnp.int32, (GC, 2 * LANE), 1) < LANE)

    def hg_chunk_out(c):
        def run():
            rs = slice(c * GC, (c + 1) * GC)
            _, e_incl, e_end = hg[("e", c)]
            cols = []
            for pr in range(2):
                ls2 = slice(pr * 2 * LANE, (pr + 1) * 2 * LANE)
                attn2 = jnp.concatenate([hg[("attn", c, 2 * pr)], hg[("attn", c, 2 * pr + 1)]], axis=1)
                vt = hg["v"][rs, pr * LANE:(pr + 1) * LANE]
                vt2 = jnp.concatenate([jnp.where(m_lo, vt, 0.0), jnp.where(m_lo, 0.0, vt)], axis=0)
                cols.append(_mm(attn2, vt2) + _mm_nt(hg["q"][rs, ls2] * e_incl[:, ls2], hg["st"][pr]))
            hg["rows"].append(jnp.concatenate(cols, axis=1))
            for pr in range(2):
                ls2 = slice(pr * 2 * LANE, (pr + 1) * 2 * LANE)
                vt = hg["v"][rs, pr * LANE:(pr + 1) * LANE]
                hg["st"][pr] = hg["st"][pr] * e_incl[GC - 1:GC, ls2] + jnp.where(
                    bd_mask, _mm_tn(vt, hg["k"][rs, ls2] * e_end[:, ls2]), 0.0)
        return run

    def hg_finish():
        for pr in range(2):
            st_hg[pr] = hg["st"][pr]
        ho = jnp.concatenate(hg["rows"], axis=0)
        ms = _segsum(ho * ho, seg64) * (1.0 / 64)
        hg["out"] = ho * lax.rsqrt(ms + NORM_EPS) * hgn_ref[...] * _silu(p_ref[:, HG_OFF + 1280:HG_OFF + 1536])

    hg_stages = [hg_gates]
    for c in range(n_gc):
        hg_stages += [hg_levels(c)] + [hg_scores(c, h) for h in range(4)] + [hg_chunk_out(c)]
    hg_stages.append(hg_finish)

    def gl_gates():
        cin = p_ref[:, GL_OFF:GL_OFF + GL_QKV]
        _push_rows(tail_conv, cin)
        cw = conv_ref[...]
        acc = cin * cw[3:4]
        for j in range(3):
            acc = acc + _shifted(tail_conv, 3 - j, TT) * cw[j:j + 1]
        _roll_history(tail_conv, TT)
        qkv = _silu(acc)
        gl["q"] = qkv[:, 0:256] * (GL_KEY_DIM ** -0.5)
        gl["k"] = qkv[:, 256:512]
        gl["v"] = qkv[:, 512:1024]
        gl["kb"] = gl["k"].astype(BF16)
        gl["log_a"] = -_softplus(-(_mm(rw["misc"], gateup_ref[...]) + gateb_ref[...])) * (1.0 / GL_GATE_NORMALIZER)
        gl["st"] = [st_gl[h] for h in range(4)]
        gl["rows"] = []

    def gl_levels(c):
        def run():
            la = gl["log_a"][c * GC:(c + 1) * GC]
            gl[("e", c)] = _decay_levels(la, _sel_mm(tri128, la))
        return run

    def gl_scores(c, h):
        def run():
            rs = slice(c * GC, (c + 1) * GC)
            ls = slice((h // 2) * LANE, (h // 2 + 1) * LANE)
            m = m_lo if h % 2 == 0 else jnp.logical_not(m_lo)
            gl[("attn", c, h)] = level_scores(jnp.where(m, gl["q"][rs, ls], 0.0).astype(BF16), gl["kb"][rs, ls],
                                              gl[("e", c)][0], ls)
        return run

    def gl_chunk_out(c):
        def run():
            rs = slice(c * GC, (c + 1) * GC)
            _, e_incl, e_end = gl[("e", c)]
            cols = []
            for h in range(4):
                ls = slice((h // 2) * LANE, (h // 2 + 1) * LANE)
                m = m_lo if h % 2 == 0 else jnp.logical_not(m_lo)
                cols.append(_mm(gl[("attn", c, h)], gl["v"][rs, h * LANE:(h + 1) * LANE])
                            + _mm_nt(jnp.where(m, gl["q"][rs, ls] * e_incl[:, ls], 0.0), gl["st"][h]))
            gl["rows"].append(jnp.concatenate(cols, axis=1))
            for h in range(4):
                ls = slice((h // 2) * LANE, (h // 2 + 1) * LANE)
                m = m_lo if h % 2 == 0 else jnp.logical_not(m_lo)
                gl["st"][h] = gl["st"][h] * e_incl[GC - 1:GC, ls] + _mm_tn(
                    gl["v"][rs, h * LANE:(h + 1) * LANE], jnp.where(m, gl["k"][rs, ls] * e_end[:, ls], 0.0))
        return run

    def gl_finish():
        for h in range(4):
            st_gl[h] = gl["st"][h]
        go = jnp.concatenate(gl["rows"], axis=0)
        ms = _segsum(go * go, seg128_ref[...]) * (1.0 / GL_VAL_DIM)
        gl["out"] = go * lax.rsqrt(ms + NORM_EPS) * gln_ref[...] * _silu(p_ref[:, GL_OG_OFF:GL_OG_OFF + 512])

    gl_stages = [gl_gates]
    for c in range(n_gc):
        gl_stages += [gl_levels(c)] + [gl_scores(c, h) for h in range(4)] + [gl_chunk_out(c)]
    gl_stages.append(gl_finish)

    _interleave(rw_stages, hg_stages, gl_stages)
    rw_out, hg_out, gl_out = rw["out"], hg["out"], gl["out"]

    mixed = jnp.concatenate([rw_out, hg_out, gl_out], axis=1).astype(BF16)
    o_ref[...] = x_ref[...] + jnp.dot(mixed, wout_ref[...], preferred_element_type=F32)


def _mix(layer, proj, x, v_first, lw, consts):
    t = x.shape[0]
    first = layer == 0
    tile = lambda n: pl.BlockSpec((TT, n), lambda i: (i, 0))
    args, in_specs = [proj, x], [tile(NP), tile(D_MODEL)]
    if not first:
        args.append(v_first)
        in_specs.append(tile(RW_W))
    names = ["rw_mu", "misc_mu", "w_up", "a_up", "g_up"] + ([] if first else ["vres_up"]) + [
        "rwp", "lb_logits", "hg_norm", "gl_conv", "gate_up", "gate_b", "gl_norm", "w_out"]
    for n in names:
        args.append(lw[n])
        in_specs.append(_const_spec(lw[n].shape) if n == "lb_logits" else _layer_spec(lw[n], layer))
    for n in ("lev", "seg64", "seg128", "tri64", "tri128"):
        args.append(consts[n])
        in_specs.append(_const_spec(consts[n].shape))
    out_shape = [jax.ShapeDtypeStruct((t, D_MODEL), F32)]
    out_specs = [tile(D_MODEL)]
    if first:
        out_shape.append(jax.ShapeDtypeStruct((t, RW_W), F32))
        out_specs.append(tile(RW_W))
    scratch = [
        pltpu.VMEM((TT + SUBLANE, RW_COLS), F32), pltpu.VMEM((TT + SUBLANE, LANE), F32),
        pltpu.VMEM((TT + SUBLANE, GL_QKV), F32),
        pltpu.VMEM((3, LANE, LANE), F32), pltpu.VMEM((2, LANE, 2 * LANE), F32), pltpu.VMEM((4, LANE, LANE), F32),
    ]
    res = pl.pallas_call(
        functools.partial(_mix_body, layer=layer),
        grid=(t // TT,),
        in_specs=in_specs,
        out_specs=out_specs,
        out_shape=out_shape,
        scratch_shapes=scratch,
        compiler_params=pltpu.CompilerParams(dimension_semantics=("arbitrary",),
                                             vmem_limit_bytes=VMEM_LIMIT),
        name="mixer0" if first else "mixer",
    )(*args)
    return (res[0], res[1]) if first else (res[0], v_first)


def _mixer_constants():
    s = np.arange(GC)[:, None]
    j = np.arange(GC)[None, :]
    x = s ^ j
    hb = np.where(x > 0, np.floor(np.log2(np.maximum(x, 1))).astype(np.int64), 0)
    lev = np.where(s > j, (N_LEVELS - 1) - hb, np.where(s == j, N_LEVELS, -1)).astype(np.int32)
    seg = lambda w: np.kron(np.eye(LANE // w, dtype=np.float32), np.ones((w, w), np.float32))
    tri64 = np.kron(np.eye(GC // RC, dtype=np.float32), np.tril(np.ones((RC, RC), np.float32)))
    tri128 = np.tril(np.ones((GC, GC), np.float32))
    return {
        "lev": jnp.asarray(lev),
        "seg64": jnp.asarray(np.concatenate([seg(64), seg(64)], axis=0), BF16),
        "seg128": jnp.asarray(np.concatenate([seg(128), seg(128)], axis=0), BF16),
        "tri64": jnp.asarray(np.concatenate([tri64, tri64], axis=1), BF16),
        "tri128": jnp.asarray(np.concatenate([tri128, tri128], axis=1), BF16),
    }


def _pad_heads(w, heads, dim, to):
    w = w.reshape(w.shape[:-1] + (heads, dim))
    w = jnp.pad(w, [(0, 0)] * (w.ndim - 1) + [(0, to - dim)])
    return w.reshape(w.shape[:-2] + (heads * to,))


def _mixer_weights(w_in, w_out, rw_mu, rw_w0, rw_w_up, rw_a0, rw_a_up, rw_g_up, rw_k_k, rw_k_a, rw_r_k,
                   rw_ln_w, rw_ln_b, rw_vres_down, rw_vres_mu, rw_v0, rw_vres_up, hg_lb_logits, hg_norm,
                   gl_conv, gl_gate_up, gl_gate_b, gl_norm):
    nl = DEPTH
    wi = w_in.astype(BF16)
    gl = wi[:, :, RW_COLS + HG_COLS:]
    q, k, v, gate_dn, og = (gl[:, :, 0:192], gl[:, :, 192:384], gl[:, :, 384:768], gl[:, :, 768:784],
                            gl[:, :, 784:1168])
    vres = jnp.concatenate([jnp.zeros((1, D_MODEL, 32), BF16), rw_vres_down.astype(BF16)], axis=0)
    vres_mu = jnp.concatenate([jnp.zeros((1, 32), F32), rw_vres_mu], axis=0)
    misc = jnp.concatenate([gate_dn, vres, jnp.zeros((nl, D_MODEL, LANE - 48), BF16)], axis=2)
    w_tail = jnp.concatenate([wi[:, :, PROJ_SPLIT:GL_OFF], _pad_heads(q, 4, 48, 64), _pad_heads(k, 4, 48, 64),
                              _pad_heads(v, 4, 96, 128), _pad_heads(og, 4, 96, 128), misc], axis=2)
    wo = w_out.astype(BF16)
    wo_gl = jnp.pad(wo[:, 640:].reshape(nl, 4, 96, D_MODEL), ((0, 0), (0, 0), (0, 32), (0, 0))).reshape(nl, 512, D_MODEL)
    zeros64 = jnp.zeros((nl, 64, RW_W), F32)
    conv_p = jnp.concatenate([_pad_heads(gl_conv[:, :, 0:192], 4, 48, 64), _pad_heads(gl_conv[:, :, 192:384], 4, 48, 64),
                              _pad_heads(gl_conv[:, :, 384:768], 4, 96, 128)], axis=2)
    v0 = jnp.concatenate([jnp.zeros((1, RW_W), F32), rw_v0], axis=0)
    vres_up = jnp.concatenate([jnp.zeros((1, 32, RW_W), F32), rw_vres_up], axis=0)
    return {
        "w_in": wi,
        "w_tail": w_tail,
        "w_out": jnp.concatenate([wo[:, :640], wo_gl], axis=1),
        "rw_mu": rw_mu[:, None],
        "misc_mu": jnp.concatenate([jnp.zeros((nl, 16), F32), vres_mu, jnp.zeros((nl, LANE - 48), F32)], axis=1)[:, None],
        "w_up": jnp.concatenate([rw_w_up, zeros64], axis=1).astype(BF16),
        "a_up": jnp.concatenate([zeros64, rw_a_up], axis=1).astype(BF16),
        "g_up": rw_g_up.astype(BF16),
        "vres_up": jnp.pad(vres_up, ((0, 0), (16, LANE - 48), (0, 0))).astype(BF16),
        "rwp": jnp.stack([rw_w0, rw_a0, rw_k_k, rw_k_a, rw_r_k.reshape(nl, RW_W), rw_ln_w, rw_ln_b, v0], axis=1),
        "lb_logits": hg_lb_logits,
        "hg_norm": hg_norm[:, None],
        "gl_conv": conv_p,
        "gate_up": jnp.pad(_pad_heads(gl_gate_up, 4, 48, 64), ((0, 0), (0, LANE - 16), (0, 0))).astype(BF16),
        "gate_b": _pad_heads(gl_gate_b, 4, 48, 64)[:, None],
        "gl_norm": _pad_heads(gl_norm, 4, 96, 128)[:, None],
    }


def kernel(x, ffn1_norm, ffn1_w_gate, ffn1_w_up, ffn1_w_down, mix_norm, w_in, w_out, rw_mu, rw_w0, rw_w_up, rw_a0, rw_a_up, rw_g_up, rw_k_k, rw_k_a, rw_r_k, rw_ln_w, rw_ln_b, rw_vres_down, rw_vres_mu, rw_v0, rw_vres_up, hg_lb_logits, hg_norm, gl_conv, gl_gate_up, gl_gate_b, gl_norm, ffn2_norm, ffn2_w_gate, ffn2_w_up, ffn2_w_down, final_norm):
    bsz, seq, d = x.shape
    assert bsz == 1 and d == D_MODEL and seq % TM_FFN == 0
    consts = _mixer_constants()
    lw = _mixer_weights(w_in, w_out, rw_mu, rw_w0, rw_w_up, rw_a0, rw_a_up, rw_g_up, rw_k_k, rw_k_a, rw_r_k,
                        rw_ln_w, rw_ln_b, rw_vres_down, rw_vres_mu, rw_v0, rw_vres_up, hg_lb_logits, hg_norm,
                        gl_conv, gl_gate_up, gl_gate_b, gl_norm)
    f1 = (ffn1_norm[:, None], ffn1_w_gate.astype(BF16), ffn1_w_up.astype(BF16), ffn1_w_down.astype(BF16))
    f2 = (ffn2_norm[:, None], ffn2_w_gate.astype(BF16), ffn2_w_up.astype(BF16), ffn2_w_down.astype(BF16))
    mix_gain = mix_norm[:, None]
    h = x.reshape(seq, d)
    v_first = None
    for l in range(DEPTH):
        h = _ffn(h, l, *f1)
        proj = _proj(h, l, mix_gain, lw["w_in"], lw["w_tail"])
        h, v_first = _mix(l, proj, h, v_first, lw, consts)
        h = _ffn(h, l, *f2, final_gain=final_norm[None] if l == DEPTH - 1 else None)
    return h.reshape(bsz, seq, d)
```

```python
import functools

import numpy as np
import jax
import jax.numpy as jnp
from jax import lax
from jax.experimental import pallas as pl
from jax.experimental.pallas import tpu as pltpu

F32 = jnp.float32
BF16 = jnp.bfloat16

D_MODEL = 1024
DEPTH = 4
D_FF = 2816
NORM_EPS = 1e-5
L2_EPS = 1e-12
RW_GN_EPS = 64e-5
HG_GATE_FLOOR = 1e-30
GL_GATE_NORMALIZER = 16.0
GL_KEY_DIM = 48
GL_VAL_DIM = 96
LOG2E = 1.4426950408889634

LANE = 128
SUBLANE = 8
VMEM_LIMIT = 56 * 1024 * 1024

RW_OFF, RW_COLS = 0, 1408
HG_OFF, HG_COLS = 1408, 1536
GL_OFF = 2944
GL_QKV = 1024
GL_OG_OFF = GL_OFF + GL_QKV
MISC_OFF = GL_OG_OFF + 512
NP = MISC_OFF + LANE
HK_OFF = NP
NPO = NP + 512
PROJ_SUB = 128
PROJ_SPLIT = 2816
RW_W = 384

TT = 512
GC = 128
RC = 64
N_LEVELS = 7
TM_FFN = 512
TM_PROJ = 512


def _mm(a, b):
    return jnp.dot(a.astype(BF16), b.astype(BF16), preferred_element_type=F32)


def _mm_nt(a, b):
    return lax.dot_general(a.astype(BF16), b.astype(BF16), (((1,), (1,)), ((), ())),
                           preferred_element_type=F32)


def _mm_tn(a, b):
    return lax.dot_general(a.astype(BF16), b.astype(BF16), (((0,), (0,)), ((), ())),
                           preferred_element_type=F32)


def _sel_mm(c2, x):
    hi = x.astype(BF16)
    lo = (x - hi.astype(F32)).astype(BF16)
    return jnp.dot(c2, jnp.concatenate([hi, lo], axis=0), preferred_element_type=F32)


def _segsum(x, seg2):
    hi = x.astype(BF16)
    lo = (x - hi.astype(F32)).astype(BF16)
    cols = [jnp.dot(jnp.concatenate([hi[:, t:t + LANE], lo[:, t:t + LANE]], axis=1), seg2,
                    preferred_element_type=F32) for t in range(0, x.shape[1], LANE)]
    return jnp.concatenate(cols, axis=1)


def _sigmoid(x):
    return 0.5 * jnp.tanh(0.5 * x) + 0.5


def _silu(x):
    h = 0.5 * x
    return h * jnp.tanh(h) + h


def _softplus(x):
    return jnp.maximum(x, 0.0) + jnp.log(1.0 + jnp.exp(-jnp.abs(x)))


def _rms(x, gain):
    return x * lax.rsqrt(jnp.mean(x * x, axis=-1, keepdims=True) + NORM_EPS) * gain


def _push_rows(hist_ref, z, r0):
    hist_ref[SUBLANE + r0:SUBLANE + r0 + z.shape[0], :] = z


def _shifted(hist_ref, k, n, r0):
    return hist_ref[SUBLANE - k + r0:SUBLANE - k + r0 + n, :]


def _roll_history(hist_ref, n):
    hist_ref[0:SUBLANE, :] = hist_ref[n:n + SUBLANE, :]


def _interleave(*seqs):
    n = max(len(s) for s in seqs)
    for i in range(n):
        for s in seqs:
            for j in range(-(-i * len(s) // n), -(-(i + 1) * len(s) // n)):
                s[j]()


def _const_spec(shape):
    nd = len(shape)
    return pl.BlockSpec(shape, lambda i, _nd=nd: (0,) * _nd, pipeline_mode=pl.Buffered(1))


def _layer_spec(arr, layer):
    nd = arr.ndim
    return pl.BlockSpec((None,) + arr.shape[1:], lambda i, _l=layer, _nd=nd: (_l,) + (0,) * (_nd - 1),
                        pipeline_mode=pl.Buffered(1))


def _ffn_body(*refs, final):
    if final:
        x_ref, g_ref, wg_ref, wu_ref, wd_ref, fg_ref, o_ref = refs
    else:
        x_ref, g_ref, wg_ref, wu_ref, wd_ref, o_ref = refs
    x = x_ref[...]
    h = _rms(x, g_ref[...]).astype(BF16)
    gate = jnp.dot(h, wg_ref[...], preferred_element_type=F32)
    up = jnp.dot(h, wu_ref[...], preferred_element_type=F32)
    act = (_silu(gate) * up).astype(BF16)
    y = x + 0.5 * jnp.dot(act, wd_ref[...], preferred_element_type=F32)
    if final:
        y = _rms(y, fg_ref[...])
    o_ref[...] = y


def _ffn(x, layer, gain, wg, wu, wd, final_gain=None):
    t = x.shape[0]
    final = final_gain is not None
    in_specs = [pl.BlockSpec((TM_FFN, D_MODEL), lambda i: (i, 0))] + [_layer_spec(a, layer) for a in (gain, wg, wu, wd)]
    args = [x, gain, wg, wu, wd]
    if final:
        in_specs.append(_const_spec((1, D_MODEL)))
        args.append(final_gain)
    return pl.pallas_call(
        functools.partial(_ffn_body, final=final),
        grid=(t // TM_FFN,),
        in_specs=in_specs,
        out_specs=pl.BlockSpec((TM_FFN, D_MODEL), lambda i: (i, 0)),
        out_shape=jax.ShapeDtypeStruct((t, D_MODEL), F32),
        compiler_params=pltpu.CompilerParams(dimension_semantics=("arbitrary",),
                                             vmem_limit_bytes=VMEM_LIMIT),
        name="ffn_final" if final else "ffn",
    )(*args)


def _proj_body(x_ref, g_ref, wm_ref, wt_ref, rwmu_ref, miscmu_ref, conv_ref, lb_ref, o_ref,
               hist_rw, hist_misc, hist_conv, *, layer):
    @pl.when(pl.program_id(0) == 0)
    def _():
        hist_rw[0:SUBLANE, :] = jnp.zeros((SUBLANE, RW_COLS), F32)
        hist_misc[0:SUBLANE, :] = jnp.zeros((SUBLANE, LANE), F32)
        hist_conv[0:SUBLANE, :] = jnp.zeros((SUBLANE, GL_QKV), F32)

    logits = lb_ref[...]
    ex = jnp.exp(logits - jnp.max(logits, axis=0, keepdims=True))
    prob = ex / jnp.sum(ex, axis=0, keepdims=True)
    lb = jnp.zeros((1, 512), F32)
    for i in range(1, layer + 1):
        lb = lb + prob[i:i + 1]
    cw = conv_ref[...]

    for r0 in range(0, TM_PROJ, PROJ_SUB):
        rows = slice(r0, r0 + PROJ_SUB)
        h = _rms(x_ref[rows, :], g_ref[...]).astype(BF16)
        main = jnp.dot(h, wm_ref[...], preferred_element_type=F32)
        tail = jnp.dot(h, wt_ref[...], preferred_element_type=F32)
        tcol = lambda a, b, _t=tail: _t[:, a - PROJ_SPLIT:b - PROJ_SPLIT]

        rw_z = main[:, RW_OFF:RW_OFF + RW_COLS]
        _push_rows(hist_rw, rw_z, r0)
        o_ref[rows, RW_OFF:RW_OFF + RW_COLS] = rw_z + (_shifted(hist_rw, 1, PROJ_SUB, r0) - rw_z) * rwmu_ref[...]

        o_ref[rows, HG_OFF:HG_OFF + 512] = _silu(main[:, HG_OFF:HG_OFF + 512])
        hf = main[:, HG_OFF + 512:HG_OFF + 1024]
        ef = jnp.exp(-jnp.abs(hf))
        big = 1.0 / (1.0 + ef)
        small = ef * big
        pos = hf >= 0.0
        o_ref[rows, HG_OFF + 512:HG_OFF + 1024] = jnp.log(
            jnp.maximum(lb + (1.0 - lb) * jnp.where(pos, big, small), HG_GATE_FLOOR))
        o_ref[rows, HK_OFF:HK_OFF + 512] = (1.0 - lb) * jnp.where(pos, small, big)
        o_ref[rows, HG_OFF + 1024:HG_OFF + 1280] = main[:, HG_OFF + 1024:HG_OFF + 1280]
        o_ref[rows, HG_OFF + 1280:PROJ_SPLIT] = _silu(main[:, HG_OFF + 1280:PROJ_SPLIT])
        o_ref[rows, PROJ_SPLIT:GL_OFF] = _silu(tcol(PROJ_SPLIT, GL_OFF))

        cin = tcol(GL_OFF, GL_OFF + GL_QKV)
        _push_rows(hist_conv, cin, r0)
        acc = cin * cw[3:4]
        for j in range(3):
            acc = acc + _shifted(hist_conv, 3 - j, PROJ_SUB, r0) * cw[j:j + 1]
        qkv = _silu(acc)
        o_ref[rows, GL_OFF:GL_OFF + 256] = qkv[:, 0:256] * (GL_KEY_DIM ** -0.5)
        o_ref[rows, GL_OFF + 256:GL_OG_OFF] = qkv[:, 256:GL_QKV]
        o_ref[rows, GL_OG_OFF:MISC_OFF] = _silu(tcol(GL_OG_OFF, MISC_OFF))

        misc = tcol(MISC_OFF, NP)
        _push_rows(hist_misc, misc, r0)
        o_ref[rows, MISC_OFF:NP] = misc + (_shifted(hist_misc, 1, PROJ_SUB, r0) - misc) * miscmu_ref[...]
    _roll_history(hist_rw, TM_PROJ)
    _roll_history(hist_conv, TM_PROJ)
    _roll_history(hist_misc, TM_PROJ)


def _proj(x, layer, gain, lw):
    t = x.shape[0]
    main_spec = pl.BlockSpec((None, D_MODEL, PROJ_SPLIT), lambda i: (layer, 0, 0), pipeline_mode=pl.Buffered(1))
    params = [lw["w_tail"], lw["rw_mu"], lw["misc_mu"], lw["gl_conv"]]
    return pl.pallas_call(
        functools.partial(_proj_body, layer=layer),
        grid=(t // TM_PROJ,),
        in_specs=[pl.BlockSpec((TM_PROJ, D_MODEL), lambda i: (i, 0)), _layer_spec(gain, layer), main_spec]
        + [_layer_spec(a, layer) for a in params] + [_const_spec(lw["lb_logits"].shape)],
        out_specs=pl.BlockSpec((TM_PROJ, NPO), lambda i: (i, 0)),
        out_shape=jax.ShapeDtypeStruct((t, NPO), F32),
        scratch_shapes=[pltpu.VMEM((TM_PROJ + SUBLANE, RW_COLS), F32), pltpu.VMEM((TM_PROJ + SUBLANE, LANE), F32),
                        pltpu.VMEM((TM_PROJ + SUBLANE, GL_QKV), F32)],
        compiler_params=pltpu.CompilerParams(dimension_semantics=("arbitrary",),
                                             vmem_limit_bytes=VMEM_LIMIT),
        name="proj",
    )(x, gain, lw["w_in"], *params, lw["lb_logits"])


def _decay_levels(log_d, g):
    kdim = log_d.shape[1]
    g = g * LOG2E
    log_d = log_d * LOG2E
    levels = []
    b = GC // 2
    while b >= 4:
        mids = [jnp.broadcast_to(g[m:m + 1], (2 * b, kdim)) for m in range(b, GC, 2 * b)]
        g_mid = mids[0] if len(mids) == 1 else jnp.concatenate(mids, axis=0)
        levels.append(jnp.exp2(-jnp.abs(g - g_mid)).astype(BF16))
        b //= 2
    up1 = pltpu.roll(log_d, GC - 1, 0)
    up2 = pltpu.roll(log_d, GC - 2, 0)
    r = lax.broadcasted_iota(jnp.int32, log_d.shape, 0)
    r4 = r & 3
    d2 = jnp.where(r4 == 0, up1 + up2, jnp.where(r4 == 1, up1, jnp.where(r4 == 2, 0.0, log_d)))
    levels.append(jnp.exp2(d2).astype(BF16))
    levels.append(jnp.exp2(jnp.where((r & 1) == 0, up1, 0.0)).astype(BF16))
    return levels, jnp.exp2(g), jnp.exp2(g[GC - 1:GC] - g)


def _mix_body(*refs, layer):
    first = layer == 0
    it = iter(refs)
    p_ref, x_ref = next(it), next(it)
    vf_ref = None if first else next(it)
    wup_ref, aup_ref, gup_ref = next(it), next(it), next(it)
    vup_ref = None if first else next(it)
    rwp_ref, hgn_ref = next(it), next(it)
    gateup_ref, gateb_ref, gln_ref, wout_ref = next(it), next(it), next(it), next(it)
    lev_ref, seg64_ref, seg128_ref, tri64_ref, tri128_ref = next(it), next(it), next(it), next(it), next(it)
    o_ref = next(it)
    vfo_ref = next(it) if first else None
    st_rw, st_hg, st_gl = (next(it) for _ in range(3))

    @pl.when(pl.program_id(0) == 0)
    def _():
        st_rw[...] = jnp.zeros_like(st_rw)
        st_hg[...] = jnp.zeros_like(st_hg)
        st_gl[...] = jnp.zeros_like(st_gl)

    m_lo = lax.broadcasted_iota(jnp.int32, (GC, LANE), 1) < 64
    row = lax.broadcasted_iota(jnp.int32, (GC, GC), 0)
    col = lax.broadcasted_iota(jnp.int32, (GC, GC), 1)
    strict = row > col
    incl = row >= col
    eye = jnp.where(row == col, 1.0, 0.0).astype(F32)
    lev = lev_ref[...]
    seg64 = seg64_ref[...]
    n_rc = TT // RC
    n_gc = TT // GC

    tri64 = tri64_ref[...]
    tri128 = tri128_ref[...]
    lane_rc = lax.broadcasted_iota(jnp.int32, (RC, LANE), 1)
    keep_lo = jnp.where(lane_rc < 64, 1.0, 0.0).astype(BF16)
    keep_hi = jnp.where(lane_rc < 64, 0.0, 1.0).astype(BF16)

    def cut(x, c, p):
        t = x[c * RC:(c + 1) * RC, p * LANE:(p + 1) * LANE]
        return jnp.concatenate([t * keep_lo, t * keep_hi], axis=0)

    probs = [(c, p) for c in range(n_rc) for p in range(3)]
    n_pr = len(probs)
    rw, hg, gl = {}, {}, {}

    def rw_gates():
        zs = p_ref[:, RW_OFF:RW_OFF + RW_COLS]
        misc = p_ref[:, MISC_OFF:MISC_OFF + LANE]
        rw["misc"] = misc

        r = zs[:, 0:384]
        k = zs[:, 384:768]
        v = zs[:, 768:1152]
        wa = zs[:, 1152:1280]
        g_dn = zs[:, 1280:1408]
        rwp = rwp_ref[...]
        w0, a0, k_k, k_a, r_k, ln_w, ln_b, v0 = (rwp[i:i + 1] for i in range(8))
        log_w = -jnp.exp(-_softplus(-(w0 + _mm(jnp.tanh(wa), wup_ref[...]))) - 0.5)
        a = _sigmoid(a0 + _mm(wa, aup_ref[...]))
        rw["g_out"] = _mm(_sigmoid(g_dn), gup_ref[...])
        if first:
            vfo_ref[...] = v
        else:
            v = v + (vf_ref[...] - v) * _sigmoid(v0 + _mm(misc, vup_ref[...]))
        kk = k * k_k
        kk = kk * jnp.minimum(lax.rsqrt(_segsum(kk * kk, seg64)), 1.0 / L2_EPS)
        k = k * (1.0 + (a - 1.0) * k_a)
        alpha = kk * a
        rw["bonus"] = _segsum(r * k * r_k, seg64) * v
        rw["ln"] = (ln_w, ln_b)

        g = jnp.concatenate([_sel_mm(tri64, log_w[c * GC:(c + 1) * GC]) for c in range(n_gc)], axis=0)
        gs = g * LOG2E
        gs_last = jnp.concatenate(
            [jnp.broadcast_to(gs[(c + 1) * RC - 1:(c + 1) * RC], (RC, RW_W)) for c in range(n_rc)], axis=0)
        kb, ab, kkb, vb = k.astype(BF16), alpha.astype(BF16), kk.astype(BF16), v.astype(BF16)
        inv = jnp.exp2(-gs).astype(BF16)
        dec = jnp.exp2(gs_last - gs).astype(BF16)
        k_in, a_in = kb * inv, ab * inv
        beta_d = kkb * jnp.exp2(gs - log_w * LOG2E).astype(BF16)
        r_d = r.astype(BF16) * jnp.exp2(gs).astype(BF16)
        k_end, a_end = kb * dec, ab * dec
        rw["chunk_decay"] = [jnp.exp2(gs[(c + 1) * RC - 1:(c + 1) * RC]) for c in range(n_rc)]
        rw["bd"] = [cut(beta_d, c, p) for c, p in probs]
        rw["rd"] = [cut(r_d, c, p) for c, p in probs]
        rw["vs"] = [cut(vb, c, p) for c, p in probs]
        rw["kin_ain"] = [jnp.concatenate([cut(k_in, c, p), cut(a_in, c, p)], axis=0) for c, p in probs]
        rw["aend"] = [cut(a_end, c, p) for c, p in probs]
        rw["kend"] = [cut(k_end, c, p) for c, p in probs]

    def rw_scores():
        sc = [_mm_nt(jnp.concatenate([rw["bd"][i], rw["rd"][i]], axis=0), rw["kin_ain"][i]) for i in range(n_pr)]
        rw["a_bk"] = [jnp.where(strict, s[:GC, :GC], 0.0).astype(BF16) for s in sc]
        rw["a_rk"] = [jnp.where(incl, s[GC:, :GC], 0.0).astype(BF16) for s in sc]
        rw["a_ra"] = [jnp.where(incl, s[GC:, GC:], 0.0).astype(BF16) for s in sc]
        rw["pw"] = [jnp.where(strict, -s[:GC, GC:], 0.0) for s in sc]
        rw["t_inv"] = [eye + x for x in rw["pw"]]

    def rw_square():
        rw["pw"] = [_mm(x, x) for x in rw["pw"]]
        rw["t_inv"] = [t + _mm(t, x) for t, x in zip(rw["t_inv"], rw["pw"])]

    def rw_abv():
        rw["abv"] = [_mm(rw["a_bk"][i], rw["vs"][i]) for i in range(n_pr)]

    def rw_solve():
        sol = [_mm(rw["t_inv"][i], jnp.concatenate([rw["bd"][i], rw["abv"][i].astype(BF16)], axis=1))
               for i in range(n_pr)]
        rw["w_s"] = [s[:, :LANE].astype(BF16) for s in sol]
        rw["u_s"] = [s[:, LANE:].astype(BF16) for s in sol]

    def rw_qeff():
        rw["q_eff"] = [rw["rd"][i].astype(F32) - _mm(rw["a_ra"][i], rw["w_s"][i]) for i in range(n_pr)]

    def rw_intra():
        rw["o"] = [_mm(jnp.concatenate([rw["a_rk"][i], -rw["a_ra"][i]], axis=1),
                       jnp.concatenate([rw["vs"][i], rw["u_s"][i]], axis=0)) for i in range(n_pr)]

    def rw_lowrank():
        rw["lowrank"] = [_mm_tn(rw["aend"][i], rw["w_s"][i]) for i in range(n_pr)]

    def rw_update():
        rw["upd_t"] = [_mm_tn(jnp.concatenate([rw["vs"][i], -rw["u_s"][i]], axis=0),
                              jnp.concatenate([rw["kend"][i], rw["aend"][i]], axis=0)) for i in range(n_pr)]
        rw["st"] = [st_rw[p] for p in range(3)]
        rw["y"] = {}

    def rw_step(c):
        def run():
            for p in range(3):
                i = c * 3 + p
                st = rw["st"][p]
                o_s = rw["o"][i] + _mm_nt(rw["q_eff"][i], st)
                rw["y"][(c, p)] = o_s[:RC] + o_s[RC:]
                rw["st"][p] = (st * rw["chunk_decay"][c][:, p * LANE:(p + 1) * LANE]
                               - _mm_nt(st, rw["lowrank"][i]) + rw["upd_t"][i])
        return run

    def rw_finish():
        for p in range(3):
            st_rw[p] = rw["st"][p]
        y = jnp.concatenate([jnp.concatenate([rw["y"][(c, p)] for p in range(3)], axis=1) for c in range(n_rc)],
                            axis=0)
        mean = _segsum(y, seg64) * (1.0 / 64)
        yc = y - mean
        var = _segsum(yc * yc, seg64) * (1.0 / 64)
        ln_w, ln_b = rw["ln"]
        y = yc * lax.rsqrt(var + RW_GN_EPS) * ln_w + ln_b
        rw["out"] = (y + rw["bonus"]) * rw["g_out"]

    rw_stages = ([rw_gates, rw_scores] + [rw_square] * 5 + [rw_abv, rw_solve, rw_qeff, rw_intra, rw_lowrank, rw_update]
                 + [rw_step(c) for c in range(n_rc)] + [rw_finish])

    def level_scores(q_t, k_t, levels, lanes):
        attn = jnp.zeros((GC, GC), F32)
        for lvl, e in enumerate(levels):
            e = e[:, lanes]
            attn = jnp.where(lev == lvl, _mm_nt(q_t * e, k_t * e), attn)
        return jnp.where(lev == N_LEVELS, _mm_nt(q_t, k_t), attn)

    def hg_gates():
        hq = p_ref[:, HG_OFF:HG_OFF + 512]
        hg["log_f"] = p_ref[:, HG_OFF + 512:HG_OFF + 1024]
        hk = p_ref[:, HK_OFF:HK_OFF + 512]
        hg["q"], hg["k"], hg["qb"], hg["kb"] = hq, hk, hq.astype(BF16), hk.astype(BF16)
        hg["v"] = p_ref[:, HG_OFF + 1024:HG_OFF + 1280]
        hg["st"] = [st_hg[pr] for pr in range(2)]
        hg["rows"] = []

    def hg_levels(c):
        def run():
            lf = hg["log_f"][c * GC:(c + 1) * GC]
            hg[("e", c)] = _decay_levels(lf, _sel_mm(tri128, lf))
        return run

    def hg_scores(c, h):
        def run():
            rs = slice(c * GC, (c + 1) * GC)
            ls = slice(h * LANE, (h + 1) * LANE)
            hg[("attn", c, h)] = level_scores(hg["qb"][rs, ls], hg["kb"][rs, ls], hg[("e", c)][0], ls)
        return run

    bd_mask = (lax.broadcasted_iota(jnp.int32, (GC, 2 * LANE), 0) < 64) == (
        lax.broadcasted_iota(jnp.int32, (GC, 2 * LANE), 1) < LANE)

    def hg_chunk_out(c):
        def run():
            rs = slice(c * GC, (c + 1) * GC)
            _, e_incl, e_end = hg[("e", c)]
            cols = []
            for pr in range(2):
                ls2 = slice(pr * 2 * LANE, (pr + 1) * 2 * LANE)
                attn2 = jnp.concatenate([hg[("attn", c, 2 * pr)], hg[("attn", c, 2 * pr + 1)]], axis=1)
                vt = hg["v"][rs, pr * LANE:(pr + 1) * LANE]
                vt2 = jnp.concatenate([jnp.where(m_lo, vt, 0.0), jnp.where(m_lo, 0.0, vt)], axis=0)
                cols.append(_mm(attn2, vt2) + _mm_nt(hg["q"][rs, ls2] * e_incl[:, ls2], hg["st"][pr]))
            hg["rows"].append(jnp.concatenate(cols, axis=1))
            for pr in range(2):
                ls2 = slice(pr * 2 * LANE, (pr + 1) * 2 * LANE)
                vt = hg["v"][rs, pr * LANE:(pr + 1) * LANE]
                hg["st"][pr] = hg["st"][pr] * e_incl[GC - 1:GC, ls2] + jnp.where(
                    bd_mask, _mm_tn(vt, hg["k"][rs, ls2] * e_end[:, ls2]), 0.0)
        return run

    def hg_finish():
        for pr in range(2):
            st_hg[pr] = hg["st"][pr]
        ho = jnp.concatenate(hg["rows"], axis=0)
        ms = _segsum(ho * ho, seg64) * (1.0 / 64)
        hg["out"] = ho * lax.rsqrt(ms + NORM_EPS) * hgn_ref[...] * p_ref[:, HG_OFF + 1280:HG_OFF + 1536]

    hg_stages = [hg_gates]
    for c in range(n_gc):
        hg_stages += [hg_levels(c)] + [hg_scores(c, h) for h in range(4)] + [hg_chunk_out(c)]
    hg_stages.append(hg_finish)

    def gl_gates():
        gl["q"] = p_ref[:, GL_OFF:GL_OFF + 256]
        gl["k"] = p_ref[:, GL_OFF + 256:GL_OFF + 512]
        gl["v"] = p_ref[:, GL_OFF + 512:GL_OG_OFF]
        gl["kb"] = gl["k"].astype(BF16)
        gl["log_a"] = -_softplus(-(_mm(rw["misc"], gateup_ref[...]) + gateb_ref[...])) * (1.0 / GL_GATE_NORMALIZER)
        gl["st"] = [st_gl[h] for h in range(4)]
        gl["rows"] = []

    def gl_levels(c):
        def run():
            la = gl["log_a"][c * GC:(c + 1) * GC]
            gl[("e", c)] = _decay_levels(la, _sel_mm(tri128, la))
        return run

    def gl_scores(c, h):
        def run():
            rs = slice(c * GC, (c + 1) * GC)
            ls = slice((h // 2) * LANE, (h // 2 + 1) * LANE)
            m = m_lo if h % 2 == 0 else jnp.logical_not(m_lo)
            gl[("attn", c, h)] = level_scores(jnp.where(m, gl["q"][rs, ls], 0.0).astype(BF16), gl["kb"][rs, ls],
                                              gl[("e", c)][0], ls)
        return run

    def gl_chunk_out(c):
        def run():
            rs = slice(c * GC, (c + 1) * GC)
            _, e_incl, e_end = gl[("e", c)]
            cols = []
            for h in range(4):
                ls = slice((h // 2) * LANE, (h // 2 + 1) * LANE)
                m = m_lo if h % 2 == 0 else jnp.logical_not(m_lo)
                cols.append(_mm(gl[("attn", c, h)], gl["v"][rs, h * LANE:(h + 1) * LANE])
                            + _mm_nt(jnp.where(m, gl["q"][rs, ls] * e_incl[:, ls], 0.0), gl["st"][h]))
            gl["rows"].append(jnp.concatenate(cols, axis=1))
            for h in range(4):
                ls = slice((h // 2) * LANE, (h // 2 + 1) * LANE)
                m = m_lo if h % 2 == 0 else jnp.logical_not(m_lo)
                gl["st"][h] = gl["st"][h] * e_incl[GC - 1:GC, ls] + _mm_tn(
                    gl["v"][rs, h * LANE:(h + 1) * LANE], jnp.where(m, gl["k"][rs, ls] * e_end[:, ls], 0.0))
        return run

    def gl_finish():
        for h in range(4):
            st_gl[h] = gl["st"][h]
        go = jnp.concatenate(gl["rows"], axis=0)
        ms = _segsum(go * go, seg128_ref[...]) * (1.0 / GL_VAL_DIM)
        gl["out"] = go * lax.rsqrt(ms + NORM_EPS) * gln_ref[...] * p_ref[:, GL_OG_OFF:GL_OG_OFF + 512]

    gl_stages = [gl_gates]
    for c in range(n_gc):
        gl_stages += [gl_levels(c)] + [gl_scores(c, h) for h in range(4)] + [gl_chunk_out(c)]
    gl_stages.append(gl_finish)

    _interleave(rw_stages, hg_stages, gl_stages)
    rw_out, hg_out, gl_out = rw["out"], hg["out"], gl["out"]

    mixed = jnp.concatenate([rw_out, hg_out, gl_out], axis=1).astype(BF16)
    o_ref[...] = x_ref[...] + jnp.dot(mixed, wout_ref[...], preferred_element_type=F32)


def _mix(layer, proj, x, v_first, lw, consts):
    t = x.shape[0]
    first = layer == 0
    tile = lambda n: pl.BlockSpec((TT, n), lambda i: (i, 0))
    args, in_specs = [proj, x], [tile(NPO), tile(D_MODEL)]
    if not first:
        args.append(v_first)
        in_specs.append(tile(RW_W))
    names = ["w_up", "a_up", "g_up"] + ([] if first else ["vres_up"]) + [
        "rwp", "hg_norm", "gate_up", "gate_b", "gl_norm", "w_out"]
    for n in names:
        args.append(lw[n])
        in_specs.append(_layer_spec(lw[n], layer))
    for n in ("lev", "seg64", "seg128", "tri64", "tri128"):
        args.append(consts[n])
        in_specs.append(_const_spec(consts[n].shape))
    out_shape = [jax.ShapeDtypeStruct((t, D_MODEL), F32)]
    out_specs = [tile(D_MODEL)]
    if first:
        out_shape.append(jax.ShapeDtypeStruct((t, RW_W), F32))
        out_specs.append(tile(RW_W))
    scratch = [
        pltpu.VMEM((3, LANE, LANE), F32), pltpu.VMEM((2, LANE, 2 * LANE), F32), pltpu.VMEM((4, LANE, LANE), F32),
    ]
    res = pl.pallas_call(
        functools.partial(_mix_body, layer=layer),
        grid=(t // TT,),
        in_specs=in_specs,
        out_specs=out_specs,
        out_shape=out_shape,
        scratch_shapes=scratch,
        compiler_params=pltpu.CompilerParams(dimension_semantics=("arbitrary",),
                                             vmem_limit_bytes=VMEM_LIMIT),
        name="mixer0" if first else "mixer",
    )(*args)
    return (res[0], res[1]) if first else (res[0], v_first)


def _mixer_constants():
    s = np.arange(GC)[:, None]
    j = np.arange(GC)[None, :]
    x = s ^ j
    hb = np.where(x > 0, np.floor(np.log2(np.maximum(x, 1))).astype(np.int64), 0)
    lev = np.where(s > j, (N_LEVELS - 1) - hb, np.where(s == j, N_LEVELS, -1)).astype(np.int32)
    seg = lambda w: np.kron(np.eye(LANE // w, dtype=np.float32), np.ones((w, w), np.float32))
    tri64 = np.kron(np.eye(GC // RC, dtype=np.float32), np.tril(np.ones((RC, RC), np.float32)))
    tri128 = np.tril(np.ones((GC, GC), np.float32))
    return {
        "lev": jnp.asarray(lev),
        "seg64": jnp.asarray(np.concatenate([seg(64), seg(64)], axis=0), BF16),
        "seg128": jnp.asarray(np.concatenate([seg(128), seg(128)], axis=0), BF16),
        "tri64": jnp.asarray(np.concatenate([tri64, tri64], axis=1), BF16),
        "tri128": jnp.asarray(np.concatenate([tri128, tri128], axis=1), BF16),
    }


def _pad_heads(w, heads, dim, to):
    w = w.reshape(w.shape[:-1] + (heads, dim))
    w = jnp.pad(w, [(0, 0)] * (w.ndim - 1) + [(0, to - dim)])
    return w.reshape(w.shape[:-2] + (heads * to,))


def _mixer_weights(w_in, w_out, rw_mu, rw_w0, rw_w_up, rw_a0, rw_a_up, rw_g_up, rw_k_k, rw_k_a, rw_r_k,
                   rw_ln_w, rw_ln_b, rw_vres_down, rw_vres_mu, rw_v0, rw_vres_up, hg_lb_logits, hg_norm,
                   gl_conv, gl_gate_up, gl_gate_b, gl_norm):
    nl = DEPTH
    wi = w_in.astype(BF16)
    gl = wi[:, :, RW_COLS + HG_COLS:]
    q, k, v, gate_dn, og = (gl[:, :, 0:192], gl[:, :, 192:384], gl[:, :, 384:768], gl[:, :, 768:784],
                            gl[:, :, 784:1168])
    vres = jnp.concatenate([jnp.zeros((1, D_MODEL, 32), BF16), rw_vres_down.astype(BF16)], axis=0)
    vres_mu = jnp.concatenate([jnp.zeros((1, 32), F32), rw_vres_mu], axis=0)
    misc = jnp.concatenate([gate_dn, vres, jnp.zeros((nl, D_MODEL, LANE - 48), BF16)], axis=2)
    w_tail = jnp.concatenate([wi[:, :, PROJ_SPLIT:GL_OFF], _pad_heads(q, 4, 48, 64), _pad_heads(k, 4, 48, 64),
                              _pad_heads(v, 4, 96, 128), _pad_heads(og, 4, 96, 128), misc], axis=2)
    wo = w_out.astype(BF16)
    wo_gl = jnp.pad(wo[:, 640:].reshape(nl, 4, 96, D_MODEL), ((0, 0), (0, 0), (0, 32), (0, 0))).reshape(nl, 512, D_MODEL)
    zeros64 = jnp.zeros((nl, 64, RW_W), F32)
    conv_p = jnp.concatenate([_pad_heads(gl_conv[:, :, 0:192], 4, 48, 64), _pad_heads(gl_conv[:, :, 192:384], 4, 48, 64),
                              _pad_heads(gl_conv[:, :, 384:768], 4, 96, 128)], axis=2)
    v0 = jnp.concatenate([jnp.zeros((1, RW_W), F32), rw_v0], axis=0)
    vres_up = jnp.concatenate([jnp.zeros((1, 32, RW_W), F32), rw_vres_up], axis=0)
    return {
        "w_in": wi,
        "w_tail": w_tail,
        "w_out": jnp.concatenate([wo[:, :640], wo_gl], axis=1),
        "rw_mu": rw_mu[:, None],
        "misc_mu": jnp.concatenate([jnp.zeros((nl, 16), F32), vres_mu, jnp.zeros((nl, LANE - 48), F32)], axis=1)[:, None],
        "w_up": jnp.concatenate([rw_w_up, zeros64], axis=1).astype(BF16),
        "a_up": jnp.concatenate([zeros64, rw_a_up], axis=1).astype(BF16),
        "g_up": rw_g_up.astype(BF16),
        "vres_up": jnp.pad(vres_up, ((0, 0), (16, LANE - 48), (0, 0))).astype(BF16),
        "rwp": jnp.stack([rw_w0, rw_a0, rw_k_k, rw_k_a, rw_r_k.reshape(nl, RW_W), rw_ln_w, rw_ln_b, v0], axis=1),
        "lb_logits": hg_lb_logits,
        "hg_norm": hg_norm[:, None],
        "gl_conv": conv_p,
        "gate_up": jnp.pad(_pad_heads(gl_gate_up, 4, 48, 64), ((0, 0), (0, LANE - 16), (0, 0))).astype(BF16),
        "gate_b": _pad_heads(gl_gate_b, 4, 48, 64)[:, None],
        "gl_norm": _pad_heads(gl_norm, 4, 96, 128)[:, None],
    }


def kernel(x, ffn1_norm, ffn1_w_gate, ffn1_w_up, ffn1_w_down, mix_norm, w_in, w_out, rw_mu, rw_w0, rw_w_up, rw_a0, rw_a_up, rw_g_up, rw_k_k, rw_k_a, rw_r_k, rw_ln_w, rw_ln_b, rw_vres_down, rw_vres_mu, rw_v0, rw_vres_up, hg_lb_logits, hg_norm, gl_conv, gl_gate_up, gl_gate_b, gl_norm, ffn2_norm, ffn2_w_gate, ffn2_w_up, ffn2_w_down, final_norm):
    bsz, seq, d = x.shape
    assert bsz == 1 and d == D_MODEL and seq % TM_FFN == 0
    consts = _mixer_constants()
    lw = _mixer_weights(w_in, w_out, rw_mu, rw_w0, rw_w_up, rw_a0, rw_a_up, rw_g_up, rw_k_k, rw_k_a, rw_r_k,
                        rw_ln_w, rw_ln_b, rw_vres_down, rw_vres_mu, rw_v0, rw_vres_up, hg_lb_logits, hg_norm,
                        gl_conv, gl_gate_up, gl_gate_b, gl_norm)
    f1 = (ffn1_norm[:, None], ffn1_w_gate.astype(BF16), ffn1_w_up.astype(BF16), ffn1_w_down.astype(BF16))
    f2 = (ffn2_norm[:, None], ffn2_w_gate.astype(BF16), ffn2_w_up.astype(BF16), ffn2_w_down.astype(BF16))
    mix_gain = mix_norm[:, None]
    h = x.reshape(seq, d)
    v_first = None
    for l in range(DEPTH):
        h = _ffn(h, l, *f1)
        proj = _proj(h, l, mix_gain, lw)
        h, v_first = _mix(l, proj, h, v_first, lw, consts)
        h = _ffn(h, l, *f2, final_gain=final_norm[None] if l == DEPTH - 1 else None)
    return h.reshape(bsz, seq, d)
```

```python
import functools

import numpy as np
import jax
import jax.numpy as jnp
from jax import lax
from jax.experimental import pallas as pl
from jax.experimental.pallas import tpu as pltpu

F32 = jnp.float32
BF16 = jnp.bfloat16

D_MODEL = 1024
DEPTH = 4
D_FF = 2816
NORM_EPS = 1e-5
L2_EPS = 1e-12
RW_GN_EPS = 64e-5
HG_GATE_FLOOR = 1e-30
GL_GATE_NORMALIZER = 16.0
GL_KEY_DIM = 48
GL_VAL_DIM = 96
LOG2E = 1.4426950408889634

LANE = 128
SUBLANE = 8
VMEM_LIMIT = 56 * 1024 * 1024

RW_OFF, RW_COLS = 0, 1408
HG_OFF, HG_COLS = 1408, 1536
GL_OFF = 2944
GL_QKV = 1024
GL_OG_OFF = GL_OFF + GL_QKV
MISC_OFF = GL_OG_OFF + 512
NP = MISC_OFF + LANE
HK_OFF = NP
NPO = NP + 512
PROJ_SUB = 256
PROJ_SPLIT = 2816
RW_W = 384

TT = 512
GC = 128
RC = 64
N_LEVELS = 7
TM_FFN = 512
TM_PROJ = 512


def _mm(a, b):
    return jnp.dot(a.astype(BF16), b.astype(BF16), preferred_element_type=F32)


def _mm_nt(a, b):
    return lax.dot_general(a.astype(BF16), b.astype(BF16), (((1,), (1,)), ((), ())),
                           preferred_element_type=F32)


def _mm_tn(a, b):
    return lax.dot_general(a.astype(BF16), b.astype(BF16), (((0,), (0,)), ((), ())),
                           preferred_element_type=F32)


def _sel_mm(c2, x):
    hi = x.astype(BF16)
    lo = (x - hi.astype(F32)).astype(BF16)
    return jnp.dot(c2, jnp.concatenate([hi, lo], axis=0), preferred_element_type=F32)


def _segsum(x, seg2):
    hi = x.astype(BF16)
    lo = (x - hi.astype(F32)).astype(BF16)
    cols = [jnp.dot(jnp.concatenate([hi[:, t:t + LANE], lo[:, t:t + LANE]], axis=1), seg2,
                    preferred_element_type=F32) for t in range(0, x.shape[1], LANE)]
    return jnp.concatenate(cols, axis=1)


def _sigmoid(x):
    return 0.5 * jnp.tanh(0.5 * x) + 0.5


def _silu(x):
    h = 0.5 * x
    return h * jnp.tanh(h) + h


def _softplus(x):
    return jnp.maximum(x, 0.0) + jnp.log(1.0 + jnp.exp(-jnp.abs(x)))


def _rms(x, gain):
    return x * lax.rsqrt(jnp.mean(x * x, axis=-1, keepdims=True) + NORM_EPS) * gain


def _push_rows(hist_ref, z, r0):
    hist_ref[SUBLANE + r0:SUBLANE + r0 + z.shape[0], :] = z


def _shifted(hist_ref, k, n, r0):
    return hist_ref[SUBLANE - k + r0:SUBLANE - k + r0 + n, :]


def _roll_history(hist_ref, n):
    hist_ref[0:SUBLANE, :] = hist_ref[n:n + SUBLANE, :]


def _interleave(*seqs):
    n = max(len(s) for s in seqs)
    for i in range(n):
        for s in seqs:
            for j in range(-(-i * len(s) // n), -(-(i + 1) * len(s) // n)):
                s[j]()


def _const_spec(shape):
    nd = len(shape)
    return pl.BlockSpec(shape, lambda i, _nd=nd: (0,) * _nd, pipeline_mode=pl.Buffered(1))


def _layer_spec(arr, layer):
    nd = arr.ndim
    return pl.BlockSpec((None,) + arr.shape[1:], lambda i, _l=layer, _nd=nd: (_l,) + (0,) * (_nd - 1),
                        pipeline_mode=pl.Buffered(1))


def _ffn_body(*refs, final):
    if final:
        x_ref, g_ref, wg_ref, wu_ref, wd_ref, fg_ref, o_ref = refs
    else:
        x_ref, g_ref, wg_ref, wu_ref, wd_ref, o_ref = refs
    x = x_ref[...]
    h = _rms(x, g_ref[...]).astype(BF16)
    gate = jnp.dot(h, wg_ref[...], preferred_element_type=F32)
    up = jnp.dot(h, wu_ref[...], preferred_element_type=F32)
    act = (_silu(gate) * up).astype(BF16)
    y = x + 0.5 * jnp.dot(act, wd_ref[...], preferred_element_type=F32)
    if final:
        y = _rms(y, fg_ref[...])
    o_ref[...] = y


def _ffn(x, layer, gain, wg, wu, wd, final_gain=None):
    t = x.shape[0]
    final = final_gain is not None
    in_specs = [pl.BlockSpec((TM_FFN, D_MODEL), lambda i: (i, 0))] + [_layer_spec(a, layer) for a in (gain, wg, wu, wd)]
    args = [x, gain, wg, wu, wd]
    if final:
        in_specs.append(_const_spec((1, D_MODEL)))
        args.append(final_gain)
    return pl.pallas_call(
        functools.partial(_ffn_body, final=final),
        grid=(t // TM_FFN,),
        in_specs=in_specs,
        out_specs=pl.BlockSpec((TM_FFN, D_MODEL), lambda i: (i, 0)),
        out_shape=jax.ShapeDtypeStruct((t, D_MODEL), F32),
        compiler_params=pltpu.CompilerParams(dimension_semantics=("arbitrary",),
                                             vmem_limit_bytes=VMEM_LIMIT),
        name="ffn_final" if final else "ffn",
    )(*args)


def _proj_body(x_ref, g_ref, wm_ref, wt_ref, rwmu_ref, miscmu_ref, conv_ref, lb_ref, o_ref,
               hist_rw, hist_misc, hist_conv, *, layer):
    @pl.when(pl.program_id(0) == 0)
    def _():
        hist_rw[0:SUBLANE, :] = jnp.zeros((SUBLANE, RW_COLS), F32)
        hist_misc[0:SUBLANE, :] = jnp.zeros((SUBLANE, LANE), F32)
        hist_conv[0:SUBLANE, :] = jnp.zeros((SUBLANE, GL_QKV), F32)

    logits = lb_ref[...]
    ex = jnp.exp(logits - jnp.max(logits, axis=0, keepdims=True))
    prob = ex / jnp.sum(ex, axis=0, keepdims=True)
    lb = jnp.zeros((1, 512), F32)
    for i in range(1, layer + 1):
        lb = lb + prob[i:i + 1]
    cw = conv_ref[...]

    for r0 in range(0, TM_PROJ, PROJ_SUB):
        rows = slice(r0, r0 + PROJ_SUB)
        h = _rms(x_ref[rows, :], g_ref[...]).astype(BF16)
        main = jnp.dot(h, wm_ref[...], preferred_element_type=F32)
        tail = jnp.dot(h, wt_ref[...], preferred_element_type=F32)
        tcol = lambda a, b, _t=tail: _t[:, a - PROJ_SPLIT:b - PROJ_SPLIT]

        rw_z = main[:, RW_OFF:RW_OFF + RW_COLS]
        _push_rows(hist_rw, rw_z, r0)
        o_ref[rows, RW_OFF:RW_OFF + RW_COLS] = rw_z + (_shifted(hist_rw, 1, PROJ_SUB, r0) - rw_z) * rwmu_ref[...]

        o_ref[rows, HG_OFF:HG_OFF + 512] = _silu(main[:, HG_OFF:HG_OFF + 512])
        hf = main[:, HG_OFF + 512:HG_OFF + 1024]
        ef = jnp.exp(-jnp.abs(hf))
        big = 1.0 / (1.0 + ef)
        small = ef * big
        pos = hf >= 0.0
        o_ref[rows, HG_OFF + 512:HG_OFF + 1024] = jnp.log(
            jnp.maximum(lb + (1.0 - lb) * jnp.where(pos, big, small), HG_GATE_FLOOR))
        o_ref[rows, HK_OFF:HK_OFF + 512] = (1.0 - lb) * jnp.where(pos, small, big)
        o_ref[rows, HG_OFF + 1024:HG_OFF + 1280] = main[:, HG_OFF + 1024:HG_OFF + 1280]
        o_ref[rows, HG_OFF + 1280:PROJ_SPLIT] = _silu(main[:, HG_OFF + 1280:PROJ_SPLIT])
        o_ref[rows, PROJ_SPLIT:GL_OFF] = _silu(tcol(PROJ_SPLIT, GL_OFF))

        cin = tcol(GL_OFF, GL_OFF + GL_QKV)
        _push_rows(hist_conv, cin, r0)
        acc = cin * cw[3:4]
        for j in range(3):
            acc = acc + _shifted(hist_conv, 3 - j, PROJ_SUB, r0) * cw[j:j + 1]
        qkv = _silu(acc)
        o_ref[rows, GL_OFF:GL_OFF + 256] = qkv[:, 0:256] * (GL_KEY_DIM ** -0.5)
        o_ref[rows, GL_OFF + 256:GL_OG_OFF] = qkv[:, 256:GL_QKV]
        o_ref[rows, GL_OG_OFF:MISC_OFF] = _silu(tcol(GL_OG_OFF, MISC_OFF))

        misc = tcol(MISC_OFF, NP)
        _push_rows(hist_misc, misc, r0)
        o_ref[rows, MISC_OFF:NP] = misc + (_shifted(hist_misc, 1, PROJ_SUB, r0) - misc) * miscmu_ref[...]
    _roll_history(hist_rw, TM_PROJ)
    _roll_history(hist_conv, TM_PROJ)
    _roll_history(hist_misc, TM_PROJ)


def _proj(x, layer, gain, lw):
    t = x.shape[0]
    main_spec = pl.BlockSpec((None, D_MODEL, PROJ_SPLIT), lambda i: (layer, 0, 0), pipeline_mode=pl.Buffered(1))
    params = [lw["w_tail"], lw["rw_mu"], lw["misc_mu"], lw["gl_conv"]]
    return pl.pallas_call(
        functools.partial(_proj_body, layer=layer),
        grid=(t // TM_PROJ,),
        in_specs=[pl.BlockSpec((TM_PROJ, D_MODEL), lambda i: (i, 0)), _layer_spec(gain, layer), main_spec]
        + [_layer_spec(a, layer) for a in params] + [_const_spec(lw["lb_logits"].shape)],
        out_specs=pl.BlockSpec((TM_PROJ, NPO), lambda i: (i, 0)),
        out_shape=jax.ShapeDtypeStruct((t, NPO), F32),
        scratch_shapes=[pltpu.VMEM((TM_PROJ + SUBLANE, RW_COLS), F32), pltpu.VMEM((TM_PROJ + SUBLANE, LANE), F32),
                        pltpu.VMEM((TM_PROJ + SUBLANE, GL_QKV), F32)],
        compiler_params=pltpu.CompilerParams(dimension_semantics=("arbitrary",),
                                             vmem_limit_bytes=VMEM_LIMIT),
        name="proj",
    )(x, gain, lw["w_in"], *params, lw["lb_logits"])


def _decay_levels(log_d, g):
    kdim = log_d.shape[1]
    g = g * LOG2E
    log_d = log_d * LOG2E
    levels = []
    b = GC // 2
    while b >= 4:
        mids = [jnp.broadcast_to(g[m:m + 1], (2 * b, kdim)) for m in range(b, GC, 2 * b)]
        g_mid = mids[0] if len(mids) == 1 else jnp.concatenate(mids, axis=0)
        levels.append(jnp.exp2(-jnp.abs(g - g_mid)).astype(BF16))
        b //= 2
    up1 = pltpu.roll(log_d, GC - 1, 0)
    up2 = pltpu.roll(log_d, GC - 2, 0)
    r = lax.broadcasted_iota(jnp.int32, log_d.shape, 0)
    r4 = r & 3
    d2 = jnp.where(r4 == 0, up1 + up2, jnp.where(r4 == 1, up1, jnp.where(r4 == 2, 0.0, log_d)))
    levels.append(jnp.exp2(d2).astype(BF16))
    levels.append(jnp.exp2(jnp.where((r & 1) == 0, up1, 0.0)).astype(BF16))
    return levels, jnp.exp2(g), jnp.exp2(g[GC - 1:GC] - g)


def _mix_body(*refs, layer):
    first = layer == 0
    it = iter(refs)
    p_ref, x_ref = next(it), next(it)
    vf_ref = None if first else next(it)
    wup_ref, aup_ref, gup_ref = next(it), next(it), next(it)
    vup_ref = None if first else next(it)
    rwp_ref, hgn_ref = next(it), next(it)
    gateup_ref, gateb_ref, gln_ref, wout_ref = next(it), next(it), next(it), next(it)
    lev_ref, seg64_ref, seg128_ref, tri64_ref, tri128_ref = next(it), next(it), next(it), next(it), next(it)
    o_ref = next(it)
    vfo_ref = next(it) if first else None
    st_rw, st_hg, st_gl = (next(it) for _ in range(3))

    @pl.when(pl.program_id(0) == 0)
    def _():
        st_rw[...] = jnp.zeros_like(st_rw)
        st_hg[...] = jnp.zeros_like(st_hg)
        st_gl[...] = jnp.zeros_like(st_gl)

    m_lo = lax.broadcasted_iota(jnp.int32, (GC, LANE), 1) < 64
    row = lax.broadcasted_iota(jnp.int32, (GC, GC), 0)
    col = lax.broadcasted_iota(jnp.int32, (GC, GC), 1)
    strict = row > col
    incl = row >= col
    eye = jnp.where(row == col, 1.0, 0.0).astype(F32)
    lev = lev_ref[...]
    seg64 = seg64_ref[...]
    n_rc = TT // RC
    n_gc = TT // GC

    tri64 = tri64_ref[...]
    tri128 = tri128_ref[...]
    lane_rc = lax.broadcasted_iota(jnp.int32, (RC, LANE), 1)
    keep_lo = jnp.where(lane_rc < 64, 1.0, 0.0).astype(BF16)
    keep_hi = jnp.where(lane_rc < 64, 0.0, 1.0).astype(BF16)

    def cut(x, c, p):
        t = x[c * RC:(c + 1) * RC, p * LANE:(p + 1) * LANE]
        return jnp.concatenate([t * keep_lo, t * keep_hi], axis=0)

    probs = [(c, p) for c in range(n_rc) for p in range(3)]
    n_pr = len(probs)
    rw, hg, gl = {}, {}, {}

    def rw_gates():
        zs = p_ref[:, RW_OFF:RW_OFF + RW_COLS]
        misc = p_ref[:, MISC_OFF:MISC_OFF + LANE]
        rw["misc"] = misc

        r = zs[:, 0:384]
        k = zs[:, 384:768]
        v = zs[:, 768:1152]
        wa = zs[:, 1152:1280]
        g_dn = zs[:, 1280:1408]
        rwp = rwp_ref[...]
        w0, a0, k_k, k_a, r_k, ln_w, ln_b, v0 = (rwp[i:i + 1] for i in range(8))
        log_w = -jnp.exp(-_softplus(-(w0 + _mm(jnp.tanh(wa), wup_ref[...]))) - 0.5)
        a = _sigmoid(a0 + _mm(wa, aup_ref[...]))
        rw["g_out"] = _mm(_sigmoid(g_dn), gup_ref[...])
        if first:
            vfo_ref[...] = v
        else:
            v = v + (vf_ref[...] - v) * _sigmoid(v0 + _mm(misc, vup_ref[...]))
        kk = k * k_k
        kk = kk * jnp.minimum(lax.rsqrt(_segsum(kk * kk, seg64)), 1.0 / L2_EPS)
        k = k * (1.0 + (a - 1.0) * k_a)
        alpha = kk * a
        rw["bonus"] = _segsum(r * k * r_k, seg64) * v
        rw["ln"] = (ln_w, ln_b)

        g = jnp.concatenate([_sel_mm(tri64, log_w[c * GC:(c + 1) * GC]) for c in range(n_gc)], axis=0)
        gs = g * LOG2E
        gs_last = jnp.concatenate(
            [jnp.broadcast_to(gs[(c + 1) * RC - 1:(c + 1) * RC], (RC, RW_W)) for c in range(n_rc)], axis=0)
        kb, ab, kkb, vb = k.astype(BF16), alpha.astype(BF16), kk.astype(BF16), v.astype(BF16)
        inv = jnp.exp2(-gs).astype(BF16)
        dec = jnp.exp2(gs_last - gs).astype(BF16)
        k_in, a_in = kb * inv, ab * inv
        beta_d = kkb * jnp.exp2(gs - log_w * LOG2E).astype(BF16)
        r_d = r.astype(BF16) * jnp.exp2(gs).astype(BF16)
        k_end, a_end = kb * dec, ab * dec
        rw["chunk_decay"] = [jnp.exp2(gs[(c + 1) * RC - 1:(c + 1) * RC]) for c in range(n_rc)]
        rw["bd"] = [cut(beta_d, c, p) for c, p in probs]
        rw["rd"] = [cut(r_d, c, p) for c, p in probs]
        rw["vs"] = [cut(vb, c, p) for c, p in probs]
        rw["kin_ain"] = [jnp.concatenate([cut(k_in, c, p), cut(a_in, c, p)], axis=0) for c, p in probs]
        rw["aend"] = [cut(a_end, c, p) for c, p in probs]
        rw["kend"] = [cut(k_end, c, p) for c, p in probs]

    def rw_scores():
        sc = [_mm_nt(jnp.concatenate([rw["bd"][i], rw["rd"][i]], axis=0), rw["kin_ain"][i]) for i in range(n_pr)]
        rw["a_bk"] = [jnp.where(strict, s[:GC, :GC], 0.0).astype(BF16) for s in sc]
        rw["a_rk"] = [jnp.where(incl, s[GC:, :GC], 0.0).astype(BF16) for s in sc]
        rw["a_ra"] = [jnp.where(incl, s[GC:, GC:], 0.0).astype(BF16) for s in sc]
        rw["pw"] = [jnp.where(strict, -s[:GC, GC:], 0.0) for s in sc]
        rw["t_inv"] = [eye + x for x in rw["pw"]]

    def rw_square():
        rw["pw"] = [_mm(x, x) for x in rw["pw"]]
        rw["t_inv"] = [t + _mm(t, x) for t, x in zip(rw["t_inv"], rw["pw"])]

    def rw_abv():
        rw["abv"] = [_mm(rw["a_bk"][i], rw["vs"][i]) for i in range(n_pr)]

    def rw_solve():
        sol = [_mm(rw["t_inv"][i], jnp.concatenate([rw["bd"][i], rw["abv"][i].astype(BF16)], axis=1))
               for i in range(n_pr)]
        rw["w_s"] = [s[:, :LANE].astype(BF16) for s in sol]
        rw["u_s"] = [s[:, LANE:].astype(BF16) for s in sol]

    def rw_qeff():
        rw["q_eff"] = [rw["rd"][i].astype(F32) - _mm(rw["a_ra"][i], rw["w_s"][i]) for i in range(n_pr)]

    def rw_intra():
        rw["o"] = [_mm(jnp.concatenate([rw["a_rk"][i], -rw["a_ra"][i]], axis=1),
                       jnp.concatenate([rw["vs"][i], rw["u_s"][i]], axis=0)) for i in range(n_pr)]

    def rw_lowrank():
        rw["lowrank"] = [_mm_tn(rw["aend"][i], rw["w_s"][i]) for i in range(n_pr)]

    def rw_update():
        rw["upd_t"] = [_mm_tn(jnp.concatenate([rw["vs"][i], -rw["u_s"][i]], axis=0),
                              jnp.concatenate([rw["kend"][i], rw["aend"][i]], axis=0)) for i in range(n_pr)]
        rw["st"] = [st_rw[p] for p in range(3)]
        rw["y"] = {}

    def rw_step(c):
        def run():
            for p in range(3):
                i = c * 3 + p
                st = rw["st"][p]
                o_s = rw["o"][i] + _mm_nt(rw["q_eff"][i], st)
                rw["y"][(c, p)] = o_s[:RC] + o_s[RC:]
                rw["st"][p] = (st * rw["chunk_decay"][c][:, p * LANE:(p + 1) * LANE]
                               - _mm_nt(st, rw["lowrank"][i]) + rw["upd_t"][i])
        return run

    def rw_finish():
        for p in range(3):
            st_rw[p] = rw["st"][p]
        y = jnp.concatenate([jnp.concatenate([rw["y"][(c, p)] for p in range(3)], axis=1) for c in range(n_rc)],
                            axis=0)
        mean = _segsum(y, seg64) * (1.0 / 64)
        yc = y - mean
        var = _segsum(yc * yc, seg64) * (1.0 / 64)
        ln_w, ln_b = rw["ln"]
        y = yc * lax.rsqrt(var + RW_GN_EPS) * ln_w + ln_b
        rw["out"] = (y + rw["bonus"]) * rw["g_out"]

    rw_stages = ([rw_gates, rw_scores] + [rw_square] * 5 + [rw_abv, rw_solve, rw_qeff, rw_intra, rw_lowrank, rw_update]
                 + [rw_step(c) for c in range(n_rc)] + [rw_finish])

    def level_scores(q_t, k_t, levels, lanes):
        attn = jnp.zeros((GC, GC), F32)
        for lvl, e in enumerate(levels):
            e = e[:, lanes]
            attn = jnp.where(lev == lvl, _mm_nt(q_t * e, k_t * e), attn)
        return jnp.where(lev == N_LEVELS, _mm_nt(q_t, k_t), attn)

    def hg_gates():
        hq = p_ref[:, HG_OFF:HG_OFF + 512]
        hg["log_f"] = p_ref[:, HG_OFF + 512:HG_OFF + 1024]
        hk = p_ref[:, HK_OFF:HK_OFF + 512]
        hg["q"], hg["k"], hg["qb"], hg["kb"] = hq, hk, hq.astype(BF16), hk.astype(BF16)
        hg["v"] = p_ref[:, HG_OFF + 1024:HG_OFF + 1280]
        hg["st"] = [st_hg[pr] for pr in range(2)]
        hg["rows"] = []

    def hg_levels(c):
        def run():
            lf = hg["log_f"][c * GC:(c + 1) * GC]
            hg[("e", c)] = _decay_levels(lf, _sel_mm(tri128, lf))
        return run

    def hg_scores(c, h):
        def run():
            rs = slice(c * GC, (c + 1) * GC)
            ls = slice(h * LANE, (h + 1) * LANE)
            hg[("attn", c, h)] = level_scores(hg["qb"][rs, ls], hg["kb"][rs, ls], hg[("e", c)][0], ls)
        return run

    bd_mask = (lax.broadcasted_iota(jnp.int32, (GC, 2 * LANE), 0) < 64) == (
        lax.broadcasted_iota(jnp.int32, (GC, 2 * LANE), 1) < LANE)

    def hg_chunk_out(c):
        def run():
            rs = slice(c * GC, (c + 1) * GC)
            _, e_incl, e_end = hg[("e", c)]
            cols = []
            for pr in range(2):
                ls2 = slice(pr * 2 * LANE, (pr + 1) * 2 * LANE)
                attn2 = jnp.concatenate([hg[("attn", c, 2 * pr)], hg[("attn", c, 2 * pr + 1)]], axis=1)
                vt = hg["v"][rs, pr * LANE:(pr + 1) * LANE]
                vt2 = jnp.concatenate([jnp.where(m_lo, vt, 0.0), jnp.where(m_lo, 0.0, vt)], axis=0)
                cols.append(_mm(attn2, vt2) + _mm_nt(hg["q"][rs, ls2] * e_incl[:, ls2], hg["st"][pr]))
            hg["rows"].append(jnp.concatenate(cols, axis=1))
            for pr in range(2):
                ls2 = slice(pr * 2 * LANE, (pr + 1) * 2 * LANE)
                vt = hg["v"][rs, pr * LANE:(pr + 1) * LANE]
                hg["st"][pr] = hg["st"][pr] * e_incl[GC - 1:GC, ls2] + jnp.where(
                    bd_mask, _mm_tn(vt, hg["k"][rs, ls2] * e_end[:, ls2]), 0.0)
        return run

    def hg_finish():
        for pr in range(2):
            st_hg[pr] = hg["st"][pr]
        ho = jnp.concatenate(hg["rows"], axis=0)
        ms = _segsum(ho * ho, seg64) * (1.0 / 64)
        hg["out"] = ho * lax.rsqrt(ms + NORM_EPS) * hgn_ref[...] * p_ref[:, HG_OFF + 1280:HG_OFF + 1536]

    hg_stages = [hg_gates]
    for c in range(n_gc):
        hg_stages += [hg_levels(c)] + [hg_scores(c, h) for h in range(4)] + [hg_chunk_out(c)]
    hg_stages.append(hg_finish)

    def gl_gates():
        gl["q"] = p_ref[:, GL_OFF:GL_OFF + 256]
        gl["k"] = p_ref[:, GL_OFF + 256:GL_OFF + 512]
        gl["v"] = p_ref[:, GL_OFF + 512:GL_OG_OFF]
        gl["kb"] = gl["k"].astype(BF16)
        gl["log_a"] = -_softplus(-(_mm(rw["misc"], gateup_ref[...]) + gateb_ref[...])) * (1.0 / GL_GATE_NORMALIZER)
        gl["st"] = [st_gl[h] for h in range(4)]
        gl["rows"] = []

    def gl_levels(c):
        def run():
            la = gl["log_a"][c * GC:(c + 1) * GC]
            gl[("e", c)] = _decay_levels(la, _sel_mm(tri128, la))
        return run

    def gl_scores(c, h):
        def run():
            rs = slice(c * GC, (c + 1) * GC)
            ls = slice((h // 2) * LANE, (h // 2 + 1) * LANE)
            m = m_lo if h % 2 == 0 else jnp.logical_not(m_lo)
            gl[("attn", c, h)] = level_scores(jnp.where(m, gl["q"][rs, ls], 0.0).astype(BF16), gl["kb"][rs, ls],
                                              gl[("e", c)][0], ls)
        return run

    def gl_chunk_out(c):
        def run():
            rs = slice(c * GC, (c + 1) * GC)
            _, e_incl, e_end = gl[("e", c)]
            cols = []
            for h in range(4):
                ls = slice((h // 2) * LANE, (h // 2 + 1) * LANE)
                m = m_lo if h % 2 == 0 else jnp.logical_not(m_lo)
                cols.append(_mm(gl[("attn", c, h)], gl["v"][rs, h * LANE:(h + 1) * LANE])
                            + _mm_nt(jnp.where(m, gl["q"][rs, ls] * e_incl[:, ls], 0.0), gl["st"][h]))
            gl["rows"].append(jnp.concatenate(cols, axis=1))
            for h in range(4):
                ls = slice((h // 2) * LANE, (h // 2 + 1) * LANE)
                m = m_lo if h % 2 == 0 else jnp.logical_not(m_lo)
                gl["st"][h] = gl["st"][h] * e_incl[GC - 1:GC, ls] + _mm_tn(
                    gl["v"][rs, h * LANE:(h + 1) * LANE], jnp.where(m, gl["k"][rs, ls] * e_end[:, ls], 0.0))
        return run

    def gl_finish():
        for h in range(4):
            st_gl[h] = gl["st"][h]
        go = jnp.concatenate(gl["rows"], axis=0)
        ms = _segsum(go * go, seg128_ref[...]) * (1.0 / GL_VAL_DIM)
        gl["out"] = go * lax.rsqrt(ms + NORM_EPS) * gln_ref[...] * p_ref[:, GL_OG_OFF:GL_OG_OFF + 512]

    gl_stages = [gl_gates]
    for c in range(n_gc):
        gl_stages += [gl_levels(c)] + [gl_scores(c, h) for h in range(4)] + [gl_chunk_out(c)]
    gl_stages.append(gl_finish)

    _interleave(rw_stages, hg_stages, gl_stages)
    rw_out, hg_out, gl_out = rw["out"], hg["out"], gl["out"]

    mixed = jnp.concatenate([rw_out, hg_out, gl_out], axis=1).astype(BF16)
    o_ref[...] = x_ref[...] + jnp.dot(mixed, wout_ref[...], preferred_element_type=F32)


def _mix(layer, proj, x, v_first, lw, consts):
    t = x.shape[0]
    first = layer == 0
    tile = lambda n: pl.BlockSpec((TT, n), lambda i: (i, 0))
    args, in_specs = [proj, x], [tile(NPO), tile(D_MODEL)]
    if not first:
        args.append(v_first)
        in_specs.append(tile(RW_W))
    names = ["w_up", "a_up", "g_up"] + ([] if first else ["vres_up"]) + [
        "rwp", "hg_norm", "gate_up", "gate_b", "gl_norm", "w_out"]
    for n in names:
        args.append(lw[n])
        in_specs.append(_layer_spec(lw[n], layer))
    for n in ("lev", "seg64", "seg128", "tri64", "tri128"):
        args.append(consts[n])
        in_specs.append(_const_spec(consts[n].shape))
    out_shape = [jax.ShapeDtypeStruct((t, D_MODEL), F32)]
    out_specs = [tile(D_MODEL)]
    if first:
        out_shape.append(jax.ShapeDtypeStruct((t, RW_W), F32))
        out_specs.append(tile(RW_W))
    scratch = [
        pltpu.VMEM((3, LANE, LANE), F32), pltpu.VMEM((2, LANE, 2 * LANE), F32), pltpu.VMEM((4, LANE, LANE), F32),
    ]
    res = pl.pallas_call(
        functools.partial(_mix_body, layer=layer),
        grid=(t // TT,),
        in_specs=in_specs,
        out_specs=out_specs,
        out_shape=out_shape,
        scratch_shapes=scratch,
        compiler_params=pltpu.CompilerParams(dimension_semantics=("arbitrary",),
                                             vmem_limit_bytes=VMEM_LIMIT),
        name="mixer0" if first else "mixer",
    )(*args)
    return (res[0], res[1]) if first else (res[0], v_first)


def _mixer_constants():
    s = np.arange(GC)[:, None]
    j = np.arange(GC)[None, :]
    x = s ^ j
    hb = np.where(x > 0, np.floor(np.log2(np.maximum(x, 1))).astype(np.int64), 0)
    lev = np.where(s > j, (N_LEVELS - 1) - hb, np.where(s == j, N_LEVELS, -1)).astype(np.int32)
    seg = lambda w: np.kron(np.eye(LANE // w, dtype=np.float32), np.ones((w, w), np.float32))
    tri64 = np.kron(np.eye(GC // RC, dtype=np.float32), np.tril(np.ones((RC, RC), np.float32)))
    tri128 = np.tril(np.ones((GC, GC), np.float32))
    return {
        "lev": jnp.asarray(lev),
        "seg64": jnp.asarray(np.concatenate([seg(64), seg(64)], axis=0), BF16),
        "seg128": jnp.asarray(np.concatenate([seg(128), seg(128)], axis=0), BF16),
        "tri64": jnp.asarray(np.concatenate([tri64, tri64], axis=1), BF16),
        "tri128": jnp.asarray(np.concatenate([tri128, tri128], axis=1), BF16),
    }


def _pad_heads(w, heads, dim, to):
    w = w.reshape(w.shape[:-1] + (heads, dim))
    w = jnp.pad(w, [(0, 0)] * (w.ndim - 1) + [(0, to - dim)])
    return w.reshape(w.shape[:-2] + (heads * to,))


def _mixer_weights(w_in, w_out, rw_mu, rw_w0, rw_w_up, rw_a0, rw_a_up, rw_g_up, rw_k_k, rw_k_a, rw_r_k,
                   rw_ln_w, rw_ln_b, rw_vres_down, rw_vres_mu, rw_v0, rw_vres_up, hg_lb_logits, hg_norm,
                   gl_conv, gl_gate_up, gl_gate_b, gl_norm):
    nl = DEPTH
    wi = w_in.astype(BF16)
    gl = wi[:, :, RW_COLS + HG_COLS:]
    q, k, v, gate_dn, og = (gl[:, :, 0:192], gl[:, :, 192:384], gl[:, :, 384:768], gl[:, :, 768:784],
                            gl[:, :, 784:1168])
    vres = jnp.concatenate([jnp.zeros((1, D_MODEL, 32), BF16), rw_vres_down.astype(BF16)], axis=0)
    vres_mu = jnp.concatenate([jnp.zeros((1, 32), F32), rw_vres_mu], axis=0)
    misc = jnp.concatenate([gate_dn, vres, jnp.zeros((nl, D_MODEL, LANE - 48), BF16)], axis=2)
    w_tail = jnp.concatenate([wi[:, :, PROJ_SPLIT:GL_OFF], _pad_heads(q, 4, 48, 64), _pad_heads(k, 4, 48, 64),
                              _pad_heads(v, 4, 96, 128), _pad_heads(og, 4, 96, 128), misc], axis=2)
    wo = w_out.astype(BF16)
    wo_gl = jnp.pad(wo[:, 640:].reshape(nl, 4, 96, D_MODEL), ((0, 0), (0, 0), (0, 32), (0, 0))).reshape(nl, 512, D_MODEL)
    zeros64 = jnp.zeros((nl, 64, RW_W), F32)
    conv_p = jnp.concatenate([_pad_heads(gl_conv[:, :, 0:192], 4, 48, 64), _pad_heads(gl_conv[:, :, 192:384], 4, 48, 64),
                              _pad_heads(gl_conv[:, :, 384:768], 4, 96, 128)], axis=2)
    v0 = jnp.concatenate([jnp.zeros((1, RW_W), F32), rw_v0], axis=0)
    vres_up = jnp.concatenate([jnp.zeros((1, 32, RW_W), F32), rw_vres_up], axis=0)
    return {
        "w_in": wi,
        "w_tail": w_tail,
        "w_out": jnp.concatenate([wo[:, :640], wo_gl], axis=1),
        "rw_mu": rw_mu[:, None],
        "misc_mu": jnp.concatenate([jnp.zeros((nl, 16), F32), vres_mu, jnp.zeros((nl, LANE - 48), F32)], axis=1)[:, None],
        "w_up": jnp.concatenate([rw_w_up, zeros64], axis=1).astype(BF16),
        "a_up": jnp.concatenate([zeros64, rw_a_up], axis=1).astype(BF16),
        "g_up": rw_g_up.astype(BF16),
        "vres_up": jnp.pad(vres_up, ((0, 0), (16, LANE - 48), (0, 0))).astype(BF16),
        "rwp": jnp.stack([rw_w0, rw_a0, rw_k_k, rw_k_a, rw_r_k.reshape(nl, RW_W), rw_ln_w, rw_ln_b, v0], axis=1),
        "lb_logits": hg_lb_logits,
        "hg_norm": hg_norm[:, None],
        "gl_conv": conv_p,
        "gate_up": jnp.pad(_pad_heads(gl_gate_up, 4, 48, 64), ((0, 0), (0, LANE - 16), (0, 0))).astype(BF16),
        "gate_b": _pad_heads(gl_gate_b, 4, 48, 64)[:, None],
        "gl_norm": _pad_heads(gl_norm, 4, 96, 128)[:, None],
    }


def kernel(x, ffn1_norm, ffn1_w_gate, ffn1_w_up, ffn1_w_down, mix_norm, w_in, w_out, rw_mu, rw_w0, rw_w_up, rw_a0, rw_a_up, rw_g_up, rw_k_k, rw_k_a, rw_r_k, rw_ln_w, rw_ln_b, rw_vres_down, rw_vres_mu, rw_v0, rw_vres_up, hg_lb_logits, hg_norm, gl_conv, gl_gate_up, gl_gate_b, gl_norm, ffn2_norm, ffn2_w_gate, ffn2_w_up, ffn2_w_down, final_norm):
    bsz, seq, d = x.shape
    assert bsz == 1 and d == D_MODEL and seq % TM_FFN == 0
    consts = _mixer_constants()
    lw = _mixer_weights(w_in, w_out, rw_mu, rw_w0, rw_w_up, rw_a0, rw_a_up, rw_g_up, rw_k_k, rw_k_a, rw_r_k,
                        rw_ln_w, rw_ln_b, rw_vres_down, rw_vres_mu, rw_v0, rw_vres_up, hg_lb_logits, hg_norm,
                        gl_conv, gl_gate_up, gl_gate_b, gl_norm)
    f1 = (ffn1_norm[:, None], ffn1_w_gate.astype(BF16), ffn1_w_up.astype(BF16), ffn1_w_down.astype(BF16))
    f2 = (ffn2_norm[:, None], ffn2_w_gate.astype(BF16), ffn2_w_up.astype(BF16), ffn2_w_down.astype(BF16))
    mix_gain = mix_norm[:, None]
    h = x.reshape(seq, d)
    v_first = None
    for l in range(DEPTH):
        h = _ffn(h, l, *f1)
        proj = _proj(h, l, mix_gain, lw)
        h, v_first = _mix(l, proj, h, v_first, lw, consts)
        h = _ffn(h, l, *f2, final_gain=final_norm[None] if l == DEPTH - 1 else None)
    return h.reshape(bsz, seq, d)
```
